```python
import jax, jax.numpy as jnp
from jax import lax
import numpy as np

D_MODEL = 1024
BATCH = 4
SEQ = 4096
DEPTH = 4
DEC_BATCH = 128
DEC_SEQ = 1
PAST_LEN = 8192
PAGE_SIZE = 128

N_A_LAYERS = DEPTH // 2
N_B_LAYERS = DEPTH - N_A_LAYERS
H_A = 8
HD_A = 128
KVH_A = 2
G_A = H_A // KVH_A
IDX_HEADS = 16
IDX_DIM = 64
TOPK_MAX = 256
QUERY_BLOCK = 128
H_B = 16
HD_B = 64
KVH_B = 2
G_B = H_B // KVH_B
WINDOW = 128
D_FF = 2816
CONV_W = 3
ROPE_THETA = 500000.0
ROPE_FRACTION = 4
EPS = 1e-6
A_IN_SIZES = (H_A * HD_A, KVH_A * HD_A, KVH_A * HD_A, IDX_HEADS * IDX_DIM, IDX_DIM, IDX_HEADS)
A_IN_WIDTH = sum(A_IN_SIZES)

kernel_name = 'yoco_dsa_swa_sink_convglu_step'


def rms_norm(x, g):
    xf = x.astype(jnp.float32)
    y = xf * lax.rsqrt(jnp.mean(xf * xf, axis=-1, keepdims=True) + EPS)
    return (y * g.astype(jnp.float32)).astype(x.dtype)


def partial_rope(x, pos):
    hd = x.shape[-1]
    rot = hd // ROPE_FRACTION
    half = rot // 2
    inv = ROPE_THETA ** (-jnp.arange(half, dtype=jnp.float32) / half)
    ang = pos.astype(jnp.float32)[:, None] * inv[None, :]
    cos = jnp.cos(ang)[:, None, :]
    sin = jnp.sin(ang)[:, None, :]
    xf = x.astype(jnp.float32)
    x1, x2 = xf[..., :half], xf[..., half:rot]
    out = jnp.concatenate([x1 * cos - x2 * sin, x2 * cos + x1 * sin, xf[..., rot:]], axis=-1)
    return out.astype(x.dtype)


def a_project(hn, w_in, qn, kn, pos):
    Bn, T, _ = hn.shape
    offs = [sum(A_IN_SIZES[:i + 1]) for i in range(len(A_IN_SIZES) - 1)]
    q, k, v, qi, ki, wi = jnp.split(hn @ w_in, offs, axis=-1)
    q = partial_rope(rms_norm(q.reshape(Bn, T, H_A, HD_A), qn), pos)
    k = partial_rope(rms_norm(k.reshape(Bn, T, KVH_A, HD_A), kn), pos)
    v = v.reshape(Bn, T, KVH_A, HD_A)
    qi = partial_rope(qi.reshape(Bn, T, IDX_HEADS, IDX_DIM), pos)
    ki = partial_rope(ki.reshape(Bn, T, 1, IDX_DIM), pos)[:, :, 0]
    return q, k, v, qi, ki, wi


def indexer_scores(qi, wi, ki):
    dots = jnp.einsum('bthd,bsd->btsh', qi, ki, preferred_element_type=jnp.float32) * (IDX_DIM ** -0.5)
    return jnp.einsum('btsh,bth->bts', jax.nn.relu(dots), wi.astype(jnp.float32)) * (IDX_HEADS ** -0.5)


def gathered_attention(q, kg, vg, valid):
    Bn, T = q.shape[:2]
    qg = q.reshape(Bn, T, KVH_A, G_A, HD_A)
    s = jnp.einsum('btngd,btknd->btngk', qg, kg, preferred_element_type=jnp.float32) * (HD_A ** -0.5)
    s = jnp.where(valid[:, :, None, None, :], s, -jnp.inf)
    p = jax.nn.softmax(s, axis=-1).astype(vg.dtype)
    o = jnp.einsum('btngk,btknd->btngd', p, vg)
    return o.reshape(Bn, T, H_A * HD_A)


def dsa_prompt(q, k, v, qi, wi, ki):
    Bn, S = q.shape[:2]
    nb = S // QUERY_BLOCK
    topk = min(TOPK_MAX, S // 4)
    bi = jnp.arange(Bn)[:, None, None]
    key_pos = jnp.arange(S)

    def block(args):
        qb, qib, wib, t0 = args
        tpos = t0 + jnp.arange(QUERY_BLOCK)
        scores = indexer_scores(qib, wib, ki)
        scores = jnp.where((key_pos[None, :] <= tpos[:, None])[None], scores, -jnp.inf)
        _, idx = lax.top_k(scores, topk)
        valid = idx <= tpos[None, :, None]
        return gathered_attention(qb, k[bi, idx], v[bi, idx], valid)

    to_blocks = lambda a: a.reshape(Bn, nb, QUERY_BLOCK, *a.shape[2:]).swapaxes(0, 1)
    out = lax.map(block, (to_blocks(q), to_blocks(qi), to_blocks(wi), jnp.arange(nb) * QUERY_BLOCK))
    return out.swapaxes(0, 1).reshape(Bn, S, H_A * HD_A)


def dsa_sample(q, k, v, qi, wi, ki, pool_k, pool_v, pool_ki, layer, page_table):
    Bn, T = q.shape[:2]
    past = page_table.shape[1] * PAGE_SIZE
    L = past + T
    topk = min(TOPK_MAX, L // 4)
    ki_past = pool_ki[layer, page_table].reshape(Bn, past, IDX_DIM)
    scores = jnp.concatenate([indexer_scores(qi, wi, ki_past), indexer_scores(qi, wi, ki)], axis=-1)
    tpos = past + jnp.arange(T)
    scores = jnp.where((jnp.arange(L)[None, :] <= tpos[:, None])[None], scores, -jnp.inf)
    _, idx = lax.top_k(scores, topk)
    valid = idx <= tpos[None, :, None]
    in_past = (idx < past)[..., None, None]
    bi = jnp.arange(Bn)[:, None, None]
    pidx = jnp.minimum(idx, past - 1)
    phys_page = page_table[bi, pidx // PAGE_SIZE]
    prow = pidx % PAGE_SIZE
    nidx = jnp.clip(idx - past, 0, T - 1)
    kg = jnp.where(in_past, pool_k[layer, phys_page, prow], k[bi, nidx])
    vg = jnp.where(in_past, pool_v[layer, phys_page, prow], v[bi, nidx])
    return gathered_attention(q, kg, vg, valid)


def b_shared_kv(h, g, w_kv, kn, pos):
    Bn, T, _ = h.shape
    k, v = jnp.split(rms_norm(h, g) @ w_kv, 2, axis=-1)
    k = partial_rope(rms_norm(k.reshape(Bn, T, KVH_B, HD_B), kn), pos)
    return k, v.reshape(Bn, T, KVH_B, HD_B)


def b_query(hn, w_q, qn, pos):
    Bn, T, _ = hn.shape
    return partial_rope(rms_norm((hn @ w_q).reshape(Bn, T, H_B, HD_B), qn), pos)


def sink_softmax(s, sink):
    m = jnp.maximum(jnp.max(s, axis=-1, keepdims=True), sink)
    p = jnp.exp(s - m)
    return p / (jnp.sum(p, axis=-1, keepdims=True) + jnp.exp(sink - m))


def swa_prompt(q, k, v, sinks):
    Bn, S = q.shape[:2]
    nb = S // WINDOW
    qb = q.reshape(Bn, nb, WINDOW, KVH_B, G_B, HD_B)
    kb = k.reshape(Bn, nb, WINDOW, KVH_B, HD_B)
    vb = v.reshape(Bn, nb, WINDOW, KVH_B, HD_B)
    shift = lambda a: jnp.pad(a, ((0, 0), (1, 0), (0, 0), (0, 0), (0, 0)))[:, :-1]
    kband = jnp.concatenate([shift(kb), kb], axis=2)
    vband = jnp.concatenate([shift(vb), vb], axis=2)
    s = jnp.einsum('bcqngd,bcknd->bcngqk', qb, kband, preferred_element_type=jnp.float32) * (HD_B ** -0.5)
    r = jnp.arange(WINDOW)[:, None]
    c = jnp.arange(2 * WINDOW)[None, :]
    rel = c - WINDOW - r
    band = (rel <= 0) & (rel >= -WINDOW)
    has_prev = (jnp.arange(nb)[:, None, None] > 0) | (c >= WINDOW)[None]
    mask = band[None] & has_prev
    s = jnp.where(mask[None, :, None, None], s, -jnp.inf)
    sink = sinks.reshape(KVH_B, G_B)[None, None, :, :, None, None].astype(jnp.float32)
    p = sink_softmax(s, sink).astype(v.dtype)
    o = jnp.einsum('bcngqk,bcknd->bcqngd', p, vband)
    return o.reshape(Bn, S, H_B * HD_B)


def swa_sample(q, k_new, v_new, buf_k, buf_v, sinks):
    Bn, T = q.shape[:2]
    kk = jnp.concatenate([buf_k, k_new], axis=1)
    vv = jnp.concatenate([buf_v, v_new], axis=1)
    qg = q.reshape(Bn, T, KVH_B, G_B, HD_B)
    s = jnp.einsum('btngd,bknd->bngtk', qg, kk, preferred_element_type=jnp.float32) * (HD_B ** -0.5)
    kpos = jnp.concatenate([jnp.arange(WINDOW) - WINDOW, jnp.arange(T)])
    rel = kpos[None, :] - jnp.arange(T)[:, None]
    mask = (rel <= 0) & (rel >= -WINDOW)
    s = jnp.where(mask, s, -jnp.inf)
    sink = sinks.reshape(KVH_B, G_B)[None, :, :, None, None].astype(jnp.float32)
    p = sink_softmax(s, sink).astype(vv.dtype)
    o = jnp.einsum('bngtk,bknd->btngd', p, vv)
    return o.reshape(Bn, T, H_B * HD_B), kk[:, -WINDOW:], vv[:, -WINDOW:]


def conv_glu(h, g, w_up, cw, cb, w_down, prev):
    T = h.shape[1]
    gate, up = jnp.split(rms_norm(h, g) @ w_up, 2, axis=-1)
    gp = jnp.concatenate([prev, gate], axis=1)
    conv = cb + cw[0] * gp[:, 0:T]
    for j in range(1, CONV_W):
        conv = conv + cw[j] * gp[:, j:j + T]
    y = (jax.nn.silu(conv) * up) @ w_down
    return y, gp[:, -(CONV_W - 1):]


def setup_inputs(seed: int = 0) -> dict:
    key = jax.random.key(seed)
    ks = jax.random.split(key, 32)
    n_pages = PAST_LEN // PAGE_SIZE
    n_used = DEC_BATCH * n_pages
    n_pool = (5 * n_used + 3) // 4
    nrm = lambda k, shape, scale=1.0: scale * jax.random.normal(k, shape, jnp.float32)
    gain = lambda k, shape: 1.0 + 0.02 * jax.random.normal(k, shape, jnp.float32)
    page_table = jax.random.permutation(ks[8], n_pool)[:n_used].reshape(DEC_BATCH, n_pages).astype(jnp.int32)
    return {
        'x_prompt': nrm(ks[0], (BATCH, SEQ, D_MODEL)),
        'x_sample': nrm(ks[1], (DEC_BATCH, DEC_SEQ, D_MODEL)),
        'cache_a_k': nrm(ks[2], (N_A_LAYERS, n_pool, PAGE_SIZE, KVH_A, HD_A)),
        'cache_a_v': nrm(ks[3], (N_A_LAYERS, n_pool, PAGE_SIZE, KVH_A, HD_A)),
        'cache_a_kidx': nrm(ks[4], (N_A_LAYERS, n_pool, PAGE_SIZE, IDX_DIM)),
        'state_b_k': nrm(ks[5], (DEC_BATCH, WINDOW, KVH_B, HD_B)),
        'state_b_v': nrm(ks[6], (DEC_BATCH, WINDOW, KVH_B, HD_B)),
        'state_conv': nrm(ks[7], (DEPTH, DEC_BATCH, CONV_W - 1, D_FF)),
        'page_table': page_table,
        'norm_mix': gain(ks[9], (DEPTH, D_MODEL)),
        'norm_ffn': gain(ks[10], (DEPTH, D_MODEL)),
        'w_in_a': nrm(ks[11], (N_A_LAYERS, D_MODEL, A_IN_WIDTH), D_MODEL ** -0.5),
        'q_norm_a': gain(ks[12], (N_A_LAYERS, HD_A)),
        'k_norm_a': gain(ks[13], (N_A_LAYERS, HD_A)),
        'w_out_a': nrm(ks[14], (N_A_LAYERS, H_A * HD_A, D_MODEL), (H_A * HD_A) ** -0.5),
        'norm_kv_b': gain(ks[15], (D_MODEL,)),
        'w_kv_b': nrm(ks[16], (D_MODEL, 2 * KVH_B * HD_B), D_MODEL ** -0.5),
        'k_norm_b': gain(ks[17], (HD_B,)),
        'w_q_b': nrm(ks[18], (N_B_LAYERS, D_MODEL, H_B * HD_B), D_MODEL ** -0.5),
        'q_norm_b': gain(ks[19], (N_B_LAYERS, HD_B)),
        'sinks_b': nrm(ks[20], (N_B_LAYERS, H_B)),
        'w_out_b': nrm(ks[21], (N_B_LAYERS, H_B * HD_B, D_MODEL), (H_B * HD_B) ** -0.5),
        'w_up': nrm(ks[22], (DEPTH, D_MODEL, 2 * D_FF), D_MODEL ** -0.5),
        'conv_w': nrm(ks[23], (DEPTH, CONV_W, D_FF), CONV_W ** -0.5),
        'conv_b': nrm(ks[24], (DEPTH, D_FF), 0.01),
        'w_down': nrm(ks[25], (DEPTH, D_FF, D_MODEL), D_FF ** -0.5),
    }


def reference(x_prompt, x_sample, cache_a_k, cache_a_v, cache_a_kidx, state_b_k, state_b_v, state_conv,
              page_table, norm_mix, norm_ffn, w_in_a, q_norm_a, k_norm_a, w_out_a, norm_kv_b, w_kv_b,
              k_norm_b, w_q_b, q_norm_b, sinks_b, w_out_b, w_up, conv_w, conv_b, w_down):
    Bp, S, _ = x_prompt.shape
    Bs, T, _ = x_sample.shape
    past = page_table.shape[1] * PAGE_SIZE
    pos_p = jnp.arange(S, dtype=jnp.int32)
    pos_s = past + jnp.arange(T, dtype=jnp.int32)
    hp, hs = x_prompt, x_sample
    a_rows_p, a_rows_s, conv_p, conv_s = [], [], [], []
    for l in range(DEPTH):
        if l < N_A_LAYERS:
            q, k, v, qi, ki, wi = a_project(rms_norm(hp, norm_mix[l]), w_in_a[l], q_norm_a[l], k_norm_a[l], pos_p)
            hp = hp + dsa_prompt(q, k, v, qi, wi, ki) @ w_out_a[l]
            a_rows_p.append((k, v, ki))
            q, k, v, qi, ki, wi = a_project(rms_norm(hs, norm_mix[l]), w_in_a[l], q_norm_a[l], k_norm_a[l], pos_s)
            hs = hs + dsa_sample(q, k, v, qi, wi, ki, cache_a_k, cache_a_v, cache_a_kidx, l, page_table) @ w_out_a[l]
            a_rows_s.append((k, v, ki))
        else:
            b = l - N_A_LAYERS
            qp = b_query(rms_norm(hp, norm_mix[l]), w_q_b[b], q_norm_b[b], pos_p)
            hp = hp + swa_prompt(qp, kb_p, vb_p, sinks_b[b]) @ w_out_b[b]
            qs = b_query(rms_norm(hs, norm_mix[l]), w_q_b[b], q_norm_b[b], pos_s)
            o_s, new_b_k_sample, new_b_v_sample = swa_sample(qs, kb_s, vb_s, state_b_k, state_b_v, sinks_b[b])
            hs = hs + o_s @ w_out_b[b]
        yp, tail_p = conv_glu(hp, norm_ffn[l], w_up[l], conv_w[l], conv_b[l], w_down[l],
                              jnp.zeros((Bp, CONV_W - 1, D_FF), hp.dtype))
        hp = hp + yp
        ys, tail_s = conv_glu(hs, norm_ffn[l], w_up[l], conv_w[l], conv_b[l], w_down[l], state_conv[l])
        hs = hs + ys
        conv_p.append(tail_p)
        conv_s.append(tail_s)
        if l == N_A_LAYERS - 1:
            kb_p, vb_p = b_shared_kv(hp, norm_kv_b, w_kv_b, k_norm_b, pos_p)
            kb_s, vb_s = b_shared_kv(hs, norm_kv_b, w_kv_b, k_norm_b, pos_s)
    y_prompt, y_sample = hp, hs
    new_a_k_prompt = jnp.stack([r[0] for r in a_rows_p])
    new_a_v_prompt = jnp.stack([r[1] for r in a_rows_p])
    new_a_kidx_prompt = jnp.stack([r[2] for r in a_rows_p])
    new_a_k_sample = jnp.stack([r[0] for r in a_rows_s])
    new_a_v_sample = jnp.stack([r[1] for r in a_rows_s])
    new_a_kidx_sample = jnp.stack([r[2] for r in a_rows_s])
    new_b_k_prompt = kb_p[:, -WINDOW:]
    new_b_v_prompt = vb_p[:, -WINDOW:]
    new_conv_prompt = jnp.stack(conv_p)
    new_conv_sample = jnp.stack(conv_s)
    return (y_prompt, y_sample, new_a_k_prompt, new_a_v_prompt, new_a_kidx_prompt, new_a_k_sample, new_a_v_sample, new_a_kidx_sample, new_b_k_prompt, new_b_v_prompt, new_b_k_sample, new_b_v_sample, new_conv_prompt, new_conv_sample)
```

```python
import functools

import jax
import jax.numpy as jnp
from jax import lax
from jax.experimental import pallas as pl
from jax.experimental.pallas import tpu as pltpu

F32 = jnp.float32
BF16 = jnp.bfloat16
I32 = jnp.int32

D_MODEL = 1024
PAGE = 128
H_A, HD_A, KVH_A = 8, 128, 2
G_A = H_A // KVH_A
IDX_HEADS, IDX_DIM = 16, 64
TOPK_MAX = 256
QB = 128
H_B, HD_B, KVH_B = 16, 64, 2
G_B = H_B // KVH_B
WINDOW = 128
D_FF = 2816
ROPE_THETA = 500000.0
EPS = 1e-6
LANES = 128
A_Q0, A_K0, A_V0, A_QI0, A_KI0 = 0, 1024, 1280, 1536, 2560
A_WIDTH_PAD = 2688
INT_MIN = -(2 ** 31)
NEG = -1e30
VMEM_LIMIT = 56 * 1024 * 1024

_NT = (((1,), (1,)), ((), ()))


def _dot(a, b):
    return jnp.dot(a, b, preferred_element_type=F32)


def _dot_nt(a, b):
    return lax.dot_general(a, b, _NT, preferred_element_type=F32)


def _params(sem):
    return pltpu.CompilerParams(dimension_semantics=sem, vmem_limit_bytes=VMEM_LIMIT)


def _rms(x, g):
    return x * lax.rsqrt(jnp.mean(x * x, axis=-1, keepdims=True) + EPS) * g


def _head_norm(x, g, hd):
    x2 = x * x
    if hd == LANES:
        r = lax.rsqrt(jnp.sum(x2, axis=-1, keepdims=True) * (1.0 / hd) + EPS)
    else:
        lane = lax.broadcasted_iota(I32, x.shape, 1)
        lo = lane < hd
        s_lo = jnp.sum(jnp.where(lo, x2, 0.0), axis=-1, keepdims=True)
        s_hi = jnp.sum(jnp.where(lo, 0.0, x2), axis=-1, keepdims=True)
        r = jnp.where(lo, lax.rsqrt(s_lo * (1.0 / hd) + EPS), lax.rsqrt(s_hi * (1.0 / hd) + EPS))
    return x * r * g


def _rope(x, tab_ref, half):
    c, sa, sb = tab_ref[0], tab_ref[1], tab_ref[2]
    return x * c + pltpu.roll(x, LANES - half, 1) * sa + pltpu.roll(x, half, 1) * sb


def _rope_tables(pos, hd, pad_identity=0):
    rot = hd // 4
    half = rot // 2
    inv = ROPE_THETA ** (-jnp.arange(half, dtype=F32) / half)
    ang = pos.astype(F32)[:, None] * inv[None, :]
    cos, sin = jnp.cos(ang), jnp.sin(ang)
    t = pos.shape[0]
    ones = lambda n: jnp.ones((t, n), F32)
    zeros = lambda n: jnp.zeros((t, n), F32)
    c = jnp.concatenate([cos, cos, ones(hd - rot)], axis=1)
    sa = jnp.concatenate([-sin, zeros(hd - half)], axis=1)
    sb = jnp.concatenate([zeros(half), sin, zeros(hd - rot)], axis=1)
    if pad_identity:
        c = jnp.concatenate([c, ones(pad_identity)], axis=1)
        sa = jnp.concatenate([sa, zeros(pad_identity)], axis=1)
        sb = jnp.concatenate([sb, zeros(pad_identity)], axis=1)
    reps = LANES // c.shape[1]
    return jnp.stack([jnp.tile(a, (1, reps)) for a in (c, sa, sb)])


def _sort_key(x):
    bits = lax.bitcast_convert_type(x, I32)
    return jnp.where(bits >= 0, bits, bits ^ 0x7FFFFFFF)


def _kth_largest(count_ge, rows, k):
    def bit_step(i, lo):
        cand = lo ^ jnp.left_shift(jnp.int32(1), 31 - i)
        return jnp.where(count_ge(cand) >= k, cand, lo)

    return lax.fori_loop(0, 32, bit_step, jnp.full((rows, 1), INT_MIN, I32))


def _fold_lanes(m):
    part = m[:, 0:LANES]
    for t in range(1, m.shape[1] // LANES):
        part = part + m[:, t * LANES:(t + 1) * LANES]
    return part


def _proj_a_kernel(x_ref, g_ref, w_ref, qn_ref, kn_ref, t128_ref, t64_ref, tki_ref,
                   q_ref, k_ref, v_ref, kb_ref, vb_ref, qi_ref, misc_ref, kib_ref):
    xn = _rms(x_ref[...], g_ref[...]).astype(BF16)
    y = _dot(xn, w_ref[...])
    qn, kn = qn_ref[...], kn_ref[...]
    for h in range(H_A):
        xh = _head_norm(y[:, A_Q0 + h * HD_A:A_Q0 + (h + 1) * HD_A], qn, HD_A)
        q_ref[:, h * HD_A:(h + 1) * HD_A] = _rope(xh, t128_ref, HD_A // 8).astype(BF16)
    for n in range(KVH_A):
        xh = _head_norm(y[:, A_K0 + n * HD_A:A_K0 + (n + 1) * HD_A], kn, HD_A)
        kh = _rope(xh, t128_ref, HD_A // 8)
        k_ref[:, n * HD_A:(n + 1) * HD_A] = kh
        kb_ref[:, n * HD_A:(n + 1) * HD_A] = kh.astype(BF16)
    v = y[:, A_V0:A_QI0]
    v_ref[...] = v
    vb_ref[...] = v.astype(BF16)
    for c in range(IDX_HEADS * IDX_DIM // LANES):
        xh = y[:, A_QI0 + c * LANES:A_QI0 + (c + 1) * LANES]
        qi_ref[:, c * LANES:(c + 1) * LANES] = _rope(xh, t64_ref, IDX_DIM // 8).astype(BF16)
    m = _rope(y[:, A_KI0:A_WIDTH_PAD], tki_ref, IDX_DIM // 8)
    misc_ref[...] = m
    kib_ref[...] = m[:, :IDX_DIM].astype(BF16)


def _proj_a(h, g, w, qn, kn, t128, t64, tki, tm):
    m = h.shape[0]
    trows = t128.shape[1]
    nt = trows // tm
    row = lambda i: (i, 0)
    const = lambda i: (0, 0)
    tab = lambda i: (0, i % nt, 0)
    out_shapes = (
        jax.ShapeDtypeStruct((m, H_A * HD_A), BF16),
        jax.ShapeDtypeStruct((m, KVH_A * HD_A), F32),
        jax.ShapeDtypeStruct((m, KVH_A * HD_A), F32),
        jax.ShapeDtypeStruct((m, KVH_A * HD_A), BF16),
        jax.ShapeDtypeStruct((m, KVH_A * HD_A), BF16),
        jax.ShapeDtypeStruct((m, IDX_HEADS * IDX_DIM), BF16),
        jax.ShapeDtypeStruct((m, LANES), F32),
        jax.ShapeDtypeStruct((m, IDX_DIM), BF16),
    )
    return pl.pallas_call(
        _proj_a_kernel,
        grid=(m // tm,),
        in_specs=[
            pl.BlockSpec((tm, D_MODEL), row),
            pl.BlockSpec((1, D_MODEL), const),
            pl.BlockSpec((D_MODEL, A_WIDTH_PAD), const),
            pl.BlockSpec((1, LANES), const),
            pl.BlockSpec((1, LANES), const),
            pl.BlockSpec((3, tm, LANES), tab),
            pl.BlockSpec((3, tm, LANES), tab),
            pl.BlockSpec((3, tm, LANES), tab),
        ],
        out_specs=[pl.BlockSpec((tm, s.shape[1]), row) for s in out_shapes],
        out_shape=out_shapes,
        compiler_params=_params(("parallel",)),
        name="proj_a",
    )(h, g, w, qn, kn, t128, t64, tki)


def _dsa_prompt_kernel(q_ref, qi_ref, w_ref, ki_ref, kb_ref, vb_ref, o_ref, key_ref, bias_ref, *, ch, topk):
    j = pl.program_id(1)
    t0 = j * QB
    nch = (t0 + QB + ch - 1) // ch
    w = w_ref[:, IDX_DIM:IDX_DIM + IDX_HEADS] * (IDX_DIM ** -0.5 * IDX_HEADS ** -0.5)
    wcols = [w[:, h:h + 1] for h in range(IDX_HEADS)]
    qhs = [qi_ref[:, h * IDX_DIM:(h + 1) * IDX_DIM] for h in range(IDX_HEADS)]
    row = lax.broadcasted_iota(I32, (QB, ch), 0) + t0
    col = lax.broadcasted_iota(I32, (QB, ch), 1)

    def score_chunk(c, carry):
        ks = pl.multiple_of(c * ch, ch)
        kic = ki_ref[pl.ds(ks, ch), :]
        acc = jnp.zeros((QB, ch), F32)
        for h in range(IDX_HEADS):
            acc = acc + jnp.maximum(_dot_nt(qhs[h], kic), 0.0) * wcols[h]
        key_ref[c] = jnp.where(col + ks <= row, _sort_key(acc), INT_MIN)
        return carry

    lax.fori_loop(0, nch, score_chunk, 0)

    def count_ge(cand):
        def body(c, cnt):
            return cnt + _fold_lanes(jnp.where(key_ref[c] >= cand, 1.0, 0.0))

        cnt = lax.fori_loop(0, nch, body, jnp.zeros((QB, LANES), F32))
        return jnp.sum(cnt, axis=1, keepdims=True)

    thr = jnp.maximum(_kth_largest(count_ge, QB, topk), INT_MIN + 1)

    def bias_chunk(c, carry):
        bias_ref[c] = jnp.where(key_ref[c] >= thr, 0.0, NEG)
        return carry

    lax.fori_loop(0, nch, bias_chunk, 0)

    scale = HD_A ** -0.5
    for n in range(KVH_A):
        qn = jnp.concatenate(
            [q_ref[:, (n * G_A + g) * HD_A:(n * G_A + g + 1) * HD_A] for g in range(G_A)], axis=0)

        def att_chunk(c, carry, n=n, qn=qn):
            m, l, acc = carry
            ks = pl.multiple_of(c * ch, ch)
            kc = kb_ref[pl.ds(ks, ch), n * HD_A:(n + 1) * HD_A]
            vc = vb_ref[pl.ds(ks, ch), n * HD_A:(n + 1) * HD_A]
            s = _dot_nt(qn, kc) * scale
            s = (s.reshape(G_A, QB, ch) + bias_ref[c][None]).reshape(G_A * QB, ch)
            m_new = jnp.maximum(m, jnp.max(s, axis=-1, keepdims=True))
            p = jnp.exp(s - m_new)
            alpha = jnp.exp(m - m_new)
            l = alpha * l + jnp.sum(p, axis=-1, keepdims=True)
            acc = alpha * acc + _dot(p.astype(BF16), vc)
            return m_new, l, acc

        init = (jnp.full((G_A * QB, 1), NEG, F32), jnp.zeros((G_A * QB, 1), F32),
                jnp.zeros((G_A * QB, HD_A), F32))
        _, l, acc = lax.fori_loop(0, nch, att_chunk, init)
        out = acc / l
        for g in range(G_A):
            h = n * G_A + g
            o_ref[:, h * HD_A:(h + 1) * HD_A] = out[g * QB:(g + 1) * QB].astype(BF16)


def _dsa_prompt(q, qi, misc, kib, kb, vb, nbatch, seq):
    nq = seq // QB
    ch = min(512, seq)
    topk = min(TOPK_MAX, seq // 4)
    qblk = lambda b, j: (b * nq + j, 0)
    bat = lambda b, j: (b, 0)
    return pl.pallas_call(
        functools.partial(_dsa_prompt_kernel, ch=ch, topk=topk),
        grid=(nbatch, nq),
        in_specs=[
            pl.BlockSpec((QB, H_A * HD_A), qblk),
            pl.BlockSpec((QB, IDX_HEADS * IDX_DIM), qblk),
            pl.BlockSpec((QB, LANES), qblk),
            pl.BlockSpec((seq, IDX_DIM), bat),
            pl.BlockSpec((seq, KVH_A * HD_A), bat),
            pl.BlockSpec((seq, KVH_A * HD_A), bat),
        ],
        out_specs=pl.BlockSpec((QB, H_A * HD_A), qblk),
        out_shape=jax.ShapeDtypeStruct((nbatch * seq, H_A * HD_A), BF16),
        scratch_shapes=[pltpu.VMEM((seq // ch, QB, ch), I32), pltpu.VMEM((seq // ch, QB, ch), F32)],
        compiler_params=_params(("parallel", "parallel")),
        name="dsa_prompt",
    )(q, qi, misc, kib, kb, vb)


def _out_res_kernel(o_ref, w_ref, h_ref, y_ref):
    y_ref[...] = h_ref[...] + _dot(o_ref[...].astype(BF16), w_ref[...])


def _out_res(o, w, h, tm):
    m, kdim = o.shape
    row = lambda i: (i, 0)
    return pl.pallas_call(
        _out_res_kernel,
        grid=(m // tm,),
        in_specs=[pl.BlockSpec((tm, kdim), row), pl.BlockSpec((kdim, D_MODEL), lambda i: (0, 0)),
                  pl.BlockSpec((tm, D_MODEL), row)],
        out_specs=pl.BlockSpec((tm, D_MODEL), row),
        out_shape=jax.ShapeDtypeStruct((m, D_MODEL), F32),
        compiler_params=_params(("parallel",)),
        name="out_res",
    )(o, w, h)


HALO = 16


def _silu(x):
    return x * jax.nn.sigmoid(x)


def _mlp_prompt_kernel(h_ref, halo_ref, g_ref, wg_ref, wu_ref, cw_ref, cb_ref, wd_ref,
                       out_ref, tail_ref, xn_ref, acc_ref, gs_ref, *, tm, tiles_per_seq):
    i = pl.program_id(0)
    j = pl.program_id(1)

    @pl.when(j == 0)
    def _():
        g = g_ref[...]
        xn_ref[0:HALO, :] = _rms(halo_ref[...], g).astype(BF16)
        xn_ref[HALO:HALO + tm, :] = _rms(h_ref[...], g).astype(BF16)
        acc_ref[...] = jnp.zeros_like(acc_ref)

    gs_ref[...] = _dot(xn_ref[...], wg_ref[...])
    up = _dot(xn_ref[HALO:HALO + tm, :], wu_ref[...])

    @pl.when(i % tiles_per_seq == 0)
    def _():
        gs_ref[0:HALO, :] = jnp.zeros((HALO, gs_ref.shape[1]), F32)

    conv = (cb_ref[...] + cw_ref[0:1, :] * gs_ref[HALO - 2:HALO - 2 + tm, :]
            + cw_ref[1:2, :] * gs_ref[HALO - 1:HALO - 1 + tm, :]
            + cw_ref[2:3, :] * gs_ref[HALO:HALO + tm, :])
    act = (_silu(conv) * up).astype(BF16)
    acc_ref[...] += _dot(act, wd_ref[...])
    tail_ref[0] = gs_ref[tm:tm + HALO, :]

    @pl.when(j == pl.num_programs(1) - 1)
    def _():
        out_ref[...] = h_ref[...] + acc_ref[...]


def _mlp_prompt(h, g, w_up, cw, cb, w_down, seq, tm, tf):
    m = h.shape[0]
    nf = D_FF // tf
    tiles_per_seq = seq // tm
    hb = tm // HALO
    out, tails = pl.pallas_call(
        functools.partial(_mlp_prompt_kernel, tm=tm, tiles_per_seq=tiles_per_seq),
        grid=(m // tm, nf),
        in_specs=[
            pl.BlockSpec((tm, D_MODEL), lambda i, j: (i, 0)),
            pl.BlockSpec((HALO, D_MODEL), lambda i, j: (jnp.maximum(i * hb - 1, 0), 0)),
            pl.BlockSpec((1, D_MODEL), lambda i, j: (0, 0)),
            pl.BlockSpec((D_MODEL, tf), lambda i, j: (0, j)),
            pl.BlockSpec((D_MODEL, tf), lambda i, j: (0, nf + j)),
            pl.BlockSpec((3, tf), lambda i, j: (0, j)),
            pl.BlockSpec((1, tf), lambda i, j: (0, j)),
            pl.BlockSpec((tf, D_MODEL), lambda i, j: (j, 0)),
        ],
        out_specs=[pl.BlockSpec((tm, D_MODEL), lambda i, j: (i, 0)),
                   pl.BlockSpec((1, HALO, tf), lambda i, j: (i, 0, j))],
        out_shape=(jax.ShapeDtypeStruct((m, D_MODEL), F32),
                   jax.ShapeDtypeStruct((m // tm, HALO, D_FF), F32)),
        scratch_shapes=[pltpu.VMEM((tm + HALO, D_MODEL), BF16), pltpu.VMEM((tm, D_MODEL), F32),
                        pltpu.VMEM((tm + HALO, tf), F32)],
        compiler_params=_params(("parallel", "arbitrary")),
        name="mlp_prompt",
    )(h, h, g, w_up, w_up, cw, cb, w_down)
    nb = m // seq
    tail = tails.reshape(nb, tiles_per_seq, HALO, D_FF)[:, -1, HALO - 2:, :]
    return out, tail


def _mlp_sample_kernel(h_ref, s0_ref, s1_ref, g_ref, wg_ref, wu_ref, cw_ref, cb_ref, wd_ref,
                       out_ref, gate_ref, xn_ref, acc_ref):
    j = pl.program_id(0)

    @pl.when(j == 0)
    def _():
        xn_ref[...] = _rms(h_ref[...], g_ref[...]).astype(BF16)
        acc_ref[...] = jnp.zeros_like(acc_ref)

    xn = xn_ref[...]
    gate = _dot(xn, wg_ref[...])
    up = _dot(xn, wu_ref[...])
    gate_ref[...] = gate
    conv = cb_ref[...] + cw_ref[0:1, :] * s0_ref[...] + cw_ref[1:2, :] * s1_ref[...] + cw_ref[2:3, :] * gate
    act = (_silu(conv) * up).astype(BF16)
    acc_ref[...] += _dot(act, wd_ref[...])

    @pl.when(j == pl.num_programs(0) - 1)
    def _():
        out_ref[...] = h_ref[...] + acc_ref[...]


def _mlp_sample(h, s0, s1, g, w_up, cw, cb, w_down, tf):
    m = h.shape[0]
    nf = D_FF // tf
    full = lambda j: (0, 0)
    col = lambda j: (0, j)
    return pl.pallas_call(
        _mlp_sample_kernel,
        grid=(nf,),
        in_specs=[
            pl.BlockSpec((m, D_MODEL), full),
            pl.BlockSpec((m, tf), col),
            pl.BlockSpec((m, tf), col),
            pl.BlockSpec((1, D_MODEL), full),
            pl.BlockSpec((D_MODEL, tf), col),
            pl.BlockSpec((D_MODEL, tf), lambda j: (0, nf + j)),
            pl.BlockSpec((3, tf), col),
            pl.BlockSpec((1, tf), col),
            pl.BlockSpec((tf, D_MODEL), lambda j: (j, 0)),
        ],
        out_specs=[pl.BlockSpec((m, D_MODEL), full), pl.BlockSpec((m, tf), col)],
        out_shape=(jax.ShapeDtypeStruct((m, D_MODEL), F32), jax.ShapeDtypeStruct((m, D_FF), F32)),
        scratch_shapes=[pltpu.VMEM((m, D_MODEL), BF16), pltpu.VMEM((m, D_MODEL), F32)],
        compiler_params=_params(("arbitrary",)),
        name="mlp_sample",
    )(h, s0, s1, g, w_up, w_up, cw, cb, w_down)


def _kv_b_kernel(x_ref, g_ref, w_ref, kn_ref, t64_ref, k_ref, v_ref, kb_ref, vb_ref):
    xn = _rms(x_ref[...], g_ref[...]).astype(BF16)
    y = _dot(xn, w_ref[...])
    k = _rope(_head_norm(y[:, :LANES], kn_ref[...], HD_B), t64_ref, HD_B // 8)
    v = y[:, LANES:]
    k_ref[...] = k
    v_ref[...] = v
    kb_ref[...] = k.astype(BF16)
    vb_ref[...] = v.astype(BF16)


def _kv_b(h, g, w, kn, t64, tm):
    m = h.shape[0]
    nt = t64.shape[1] // tm
    row = lambda i: (i, 0)
    const = lambda i: (0, 0)
    width = KVH_B * HD_B
    shapes = (jax.ShapeDtypeStruct((m, width), F32), jax.ShapeDtypeStruct((m, width), F32),
              jax.ShapeDtypeStruct((m, width), BF16), jax.ShapeDtypeStruct((m, width), BF16))
    return pl.pallas_call(
        _kv_b_kernel,
        grid=(m // tm,),
        in_specs=[pl.BlockSpec((tm, D_MODEL), row), pl.BlockSpec((1, D_MODEL), const),
                  pl.BlockSpec((D_MODEL, 2 * width), const), pl.BlockSpec((1, LANES), const),
                  pl.BlockSpec((3, tm, LANES), lambda i: (0, i % nt, 0))],
        out_specs=[pl.BlockSpec((tm, width), row)] * 4,
        out_shape=shapes,
        compiler_params=_params(("parallel",)),
        name="kv_b",
    )(h, g, w, kn, t64)


def _q_b_kernel(x_ref, g_ref, w_ref, qn_ref, t64_ref, q_ref):
    xn = _rms(x_ref[...], g_ref[...]).astype(BF16)
    y = _dot(xn, w_ref[...])
    qn = qn_ref[...]
    for c in range(H_B * HD_B // LANES):
        xh = _head_norm(y[:, c * LANES:(c + 1) * LANES], qn, HD_B)
        q_ref[:, c * LANES:(c + 1) * LANES] = (_rope(xh, t64_ref, HD_B // 8) * (HD_B ** -0.5)).astype(q_ref.dtype)


def _q_b(h, g, w, qn, t64, tm, dtype):
    m = h.shape[0]
    nt = t64.shape[1] // tm
    row = lambda i: (i, 0)
    const = lambda i: (0, 0)
    return pl.pallas_call(
        _q_b_kernel,
        grid=(m // tm,),
        in_specs=[pl.BlockSpec((tm, D_MODEL), row), pl.BlockSpec((1, D_MODEL), const),
                  pl.BlockSpec((D_MODEL, H_B * HD_B), const), pl.BlockSpec((1, LANES), const),
                  pl.BlockSpec((3, tm, LANES), lambda i: (0, i % nt, 0))],
        out_specs=pl.BlockSpec((tm, H_B * HD_B), row),
        out_shape=jax.ShapeDtypeStruct((m, H_B * HD_B), dtype),
        compiler_params=_params(("parallel",)),
        name="q_b",
    )(h, g, w, qn, t64)


def _swa_prompt_kernel(q_ref, kc_ref, kp_ref, vc_ref, vp_ref, sink_ref, o_ref):
    j = pl.program_id(1)
    r = lax.broadcasted_iota(I32, (WINDOW, 2 * WINDOW), 0)
    c = lax.broadcasted_iota(I32, (WINDOW, 2 * WINDOW), 1)
    rel = c - WINDOW - r
    mask = (rel <= 0) & (rel >= -WINDOW) & ((c >= WINDOW) | (j > 0))
    kband = jnp.concatenate([kp_ref[...], kc_ref[...]], axis=0)
    vband = jnp.concatenate([vp_ref[...], vc_ref[...]], axis=0)
    outs = []
    for h in range(H_B):
        n = h // G_B
        qh = q_ref[:, h * HD_B:(h + 1) * HD_B]
        s = _dot_nt(qh, kband[:, n * HD_B:(n + 1) * HD_B])
        s = jnp.where(mask, s, -jnp.inf)
        sink = sink_ref[h:h + 1, 0:1]
        m = jnp.maximum(jnp.max(s, axis=-1, keepdims=True), sink)
        p = jnp.exp(s - m)
        denom = jnp.sum(p, axis=-1, keepdims=True) + jnp.exp(sink - m)
        p = (p / denom).astype(BF16)
        outs.append(_dot(p, vband[:, n * HD_B:(n + 1) * HD_B]))
    for cblk in range(H_B // 2):
        o_ref[:, cblk * LANES:(cblk + 1) * LANES] = jnp.concatenate(
            [outs[2 * cblk], outs[2 * cblk + 1]], axis=1).astype(BF16)


def _swa_prompt(q, kb, vb, sinks, nbatch, seq):
    nblk = seq // WINDOW
    cur = lambda b, j: (b * nblk + j, 0)
    prev = lambda b, j: (b * nblk + jnp.maximum(j - 1, 0), 0)
    width = KVH_B * HD_B
    return pl.pallas_call(
        _swa_prompt_kernel,
        grid=(nbatch, nblk),
        in_specs=[pl.BlockSpec((WINDOW, H_B * HD_B), cur),
                  pl.BlockSpec((WINDOW, width), cur), pl.BlockSpec((WINDOW, width), prev),
                  pl.BlockSpec((WINDOW, width), cur), pl.BlockSpec((WINDOW, width), prev),
                  pl.BlockSpec((H_B, LANES), lambda b, j: (0, 0))],
        out_specs=pl.BlockSpec((WINDOW, H_B * HD_B), cur),
        out_shape=jax.ShapeDtypeStruct((nbatch * seq, H_B * HD_B), BF16),
        compiler_params=_params(("parallel", "parallel")),
        name="swa_prompt",
    )(q, kb, kb, vb, vb, sinks)


def _swa_sample_kernel(q_ref, bk_ref, bv_ref, kn_ref, vn_ref, sink_ref, o_ref, *state_refs):
    q = q_ref[0]
    bk, bv = bk_ref[0], bv_ref[0]
    knew, vnew = kn_ref[0], vn_ref[0]
    rowh = lax.broadcasted_iota(I32, (H_B, 1), 0)
    first = rowh < G_B
    qb = q.astype(BF16)
    s_past = jnp.where(first, _dot_nt(qb, bk[:, :HD_B].astype(BF16)), _dot_nt(qb, bk[:, HD_B:].astype(BF16)))
    knew_h = jnp.where(first, knew[:, :HD_B], knew[:, HD_B:])
    vnew_h = jnp.where(first, vnew[:, :HD_B], vnew[:, HD_B:])
    s_new = jnp.sum(q * knew_h, axis=-1, keepdims=True)
    sink = sink_ref[:, 0:1]
    m = jnp.maximum(jnp.maximum(jnp.max(s_past, axis=-1, keepdims=True), s_new), sink)
    p_past = jnp.exp(s_past - m)
    p_new = jnp.exp(s_new - m)
    denom = jnp.sum(p_past, axis=-1, keepdims=True) + p_new + jnp.exp(sink - m)
    pb = (p_past / denom).astype(BF16)
    o_past = jnp.where(first, _dot(pb, bv[:, :HD_B].astype(BF16)), _dot(pb, bv[:, HD_B:].astype(BF16)))
    o_ref[0] = o_past + (p_new / denom) * vnew_h
    if state_refs:
        nk_ref, nv_ref = state_refs
        nk_ref[0] = jnp.concatenate([bk[1:], knew], axis=0)
        nv_ref[0] = jnp.concatenate([bv[1:], vnew], axis=0)


def _swa_sample(q, buf_k, buf_v, k_new, v_new, sinks, emit_state):
    nb = q.shape[0]
    width = KVH_B * HD_B
    blk3 = lambda *s: pl.BlockSpec((1,) + s, lambda b: (b, 0, 0))
    out_shape = [jax.ShapeDtypeStruct((nb, H_B, HD_B), F32)]
    out_specs = [blk3(H_B, HD_B)]
    if emit_state:
        out_shape += [jax.ShapeDtypeStruct((nb, WINDOW, width), F32)] * 2
        out_specs += [blk3(WINDOW, width)] * 2
    return pl.pallas_call(
        _swa_sample_kernel,
        grid=(nb,),
        in_specs=[blk3(H_B, HD_B), blk3(WINDOW, width), blk3(WINDOW, width), blk3(1, width), blk3(1, width),
                  pl.BlockSpec((H_B, LANES), lambda b: (0, 0))],
        out_specs=out_specs,
        out_shape=out_shape,
        compiler_params=_params(("parallel",)),
        name="swa_sample",
    )(q, buf_k, buf_v, k_new, v_new, sinks)


def _page_copies(cache_ref, layer, pt_ref, b, p0, npages, buf_ref, slot, sem):
    return [pltpu.make_async_copy(cache_ref.at[layer, pt_ref[b, p0 + p]],
                                  buf_ref.at[slot, pl.ds(p * PAGE, PAGE)], sem.at[slot])
            for p in range(npages)]


def _s_scores_kernel(pt_ref, qi_ref, w_ref, kin_ref, cache_ref, out_ref, buf_ref, sem, *, layer, npages):
    b = pl.program_id(0)
    nb = pl.num_programs(0)
    slot = b % 2

    @pl.when(b == 0)
    def _():
        for cp in _page_copies(cache_ref, layer, pt_ref, 0, 0, npages, buf_ref, 0, sem):
            cp.start()

    @pl.when(b + 1 < nb)
    def _():
        for cp in _page_copies(cache_ref, layer, pt_ref, b + 1, 0, npages, buf_ref, 1 - slot, sem):
            cp.start()

    for cp in _page_copies(cache_ref, layer, pt_ref, b, 0, npages, buf_ref, slot, sem):
        cp.wait()

    qi = qi_ref[0]
    w = w_ref[0] * (IDX_DIM ** -0.5 * IDX_HEADS ** -0.5)
    past = npages * PAGE
    d_past = _dot_nt(qi, buf_ref[slot].astype(BF16))
    d_new = _dot_nt(qi, jnp.broadcast_to(kin_ref[0], (LANES, IDX_DIM)))
    sc_past = jnp.sum(jnp.maximum(d_past, 0.0) * w, axis=0, keepdims=True)
    sc_new = jnp.sum(jnp.maximum(d_new, 0.0) * w, axis=0, keepdims=True)
    out_ref[0, :, 0:past] = sc_past
    out_ref[0, :, past:past + LANES] = sc_new


def _s_scores(page_table, qi, wi, ki_new, cache, layer):
    nb, npages = page_table.shape
    past = npages * PAGE
    blk3 = lambda *s: pl.BlockSpec((1,) + s, lambda b, pt: (b, 0, 0))
    return pl.pallas_call(
        functools.partial(_s_scores_kernel, layer=layer, npages=npages),
        grid_spec=pltpu.PrefetchScalarGridSpec(
            num_scalar_prefetch=1,
            grid=(nb,),
            in_specs=[blk3(IDX_HEADS, IDX_DIM), blk3(IDX_HEADS, 1), blk3(1, IDX_DIM),
                      pl.BlockSpec(memory_space=pl.ANY)],
            out_specs=blk3(1, past + LANES),
            scratch_shapes=[pltpu.VMEM((2, past, IDX_DIM), F32), pltpu.SemaphoreType.DMA((2,))],
        ),
        out_shape=jax.ShapeDtypeStruct((nb, 1, past + LANES), F32),
        compiler_params=_params(("arbitrary",)),
        name="sample_scores",
    )(page_table, qi, wi, ki_new, cache)


def _s_select_kernel(sc_ref, bias_ref, key_ref, *, nvalid, topk, ch):
    rows, width = sc_ref.shape
    nch = width // ch
    for c in range(nch):
        lane = lax.broadcasted_iota(I32, (rows, ch), 1) + c * ch
        key_ref[:, c * ch:(c + 1) * ch] = jnp.where(lane < nvalid, _sort_key(sc_ref[:, c * ch:(c + 1) * ch]), INT_MIN)

    def count_ge(cand):
        cnt = jnp.zeros((rows, LANES), F32)
        for c in range(nch):
            cnt = cnt + _fold_lanes(jnp.where(key_ref[:, c * ch:(c + 1) * ch] >= cand, 1.0, 0.0))
        return jnp.sum(cnt, axis=1, keepdims=True)

    thr = jnp.maximum(_kth_largest(count_ge, rows, topk), INT_MIN + 1)
    for c in range(nch):
        bias_ref[:, c * ch:(c + 1) * ch] = jnp.where(key_ref[:, c * ch:(c + 1) * ch] >= thr, 0.0, NEG)


def _s_select(sc, nvalid, topk):
    rows, width = sc.shape
    ch = LANES * 5 if width % (LANES * 5) == 0 else LANES
    return pl.pallas_call(
        functools.partial(_s_select_kernel, nvalid=nvalid, topk=topk, ch=ch),
        out_shape=jax.ShapeDtypeStruct((rows, width), F32),
        scratch_shapes=[pltpu.VMEM((rows, width), I32)],
        compiler_params=pltpu.CompilerParams(vmem_limit_bytes=VMEM_LIMIT),
        name="sample_select",
    )(sc)


def _s_attn_kernel(pt_ref, q_ref, bias_ref, bself_ref, kn_ref, vn_ref, kc_ref, vc_ref, o_ref,
                   kbuf, vbuf, ksem, vsem, m_ref, l_ref, acc_ref, *, layer, pages_per_step, nsplit):
    b = pl.program_id(0)
    hf = pl.program_id(1)
    step = b * nsplit + hf
    nsteps = pl.num_programs(0) * nsplit
    slot = step % 2

    def copies(st, sl):
        bb = st // nsplit
        p0 = (st % nsplit) * pages_per_step
        return (_page_copies(kc_ref, layer, pt_ref, bb, p0, pages_per_step, kbuf, sl, ksem)
                + _page_copies(vc_ref, layer, pt_ref, bb, p0, pages_per_step, vbuf, sl, vsem))

    @pl.when(step == 0)
    def _():
        for cp in copies(0, 0):
            cp.start()

    @pl.when(step + 1 < nsteps)
    def _():
        for cp in copies(step + 1, 1 - slot):
            cp.start()

    for cp in copies(step, slot):
        cp.wait()

    @pl.when(hf == 0)
    def _():
        m_ref[...] = jnp.full(m_ref.shape, NEG, F32)
        l_ref[...] = jnp.zeros_like(l_ref)
        acc_ref[...] = jnp.zeros_like(acc_ref)

    scale = HD_A ** -0.5
    q = q_ref[0]
    qb = q.astype(BF16)
    first = lax.broadcasted_iota(I32, (H_A, 1), 0) < G_A
    kc = kbuf[slot]
    vc = vbuf[slot]
    s = jnp.where(first, _dot_nt(qb, kc[:, :HD_A].astype(BF16)), _dot_nt(qb, kc[:, HD_A:].astype(BF16)))
    s = s * scale + bias_ref[0, 0]
    m = m_ref[...]
    m_new = jnp.maximum(m, jnp.max(s, axis=-1, keepdims=True))
    p = jnp.exp(s - m_new)
    alpha = jnp.exp(m - m_new)
    pb = p.astype(BF16)
    pv = jnp.where(first, _dot(pb, vc[:, :HD_A].astype(BF16)), _dot(pb, vc[:, HD_A:].astype(BF16)))
    l_ref[...] = alpha * l_ref[...] + jnp.sum(p, axis=-1, keepdims=True)
    acc_ref[...] = alpha * acc_ref[...] + pv
    m_ref[...] = m_new

    @pl.when(hf == nsplit - 1)
    def _():
        knew, vnew = kn_ref[0], vn_ref[0]
        knew_h = jnp.where(first, knew[:, :HD_A], knew[:, HD_A:])
        vnew_h = jnp.where(first, vnew[:, :HD_A], vnew[:, HD_A:])
        s_new = jnp.sum(q * knew_h, axis=-1, keepdims=True) * scale + bself_ref[0, :, 0:1]
        m_old = m_ref[...]
        m_fin = jnp.maximum(m_old, s_new)
        a = jnp.exp(m_old - m_fin)
        p_new = jnp.exp(s_new - m_fin)
        l = a * l_ref[...] + p_new
        o_ref[0] = (a * acc_ref[...] + p_new * vnew_h) / l


def _s_attn(page_table, q, bias_past, bias_self, k_new, v_new, cache_k, cache_v, layer):
    nb, npages = page_table.shape
    nsplit = 2 if npages % 2 == 0 else 1
    pps = npages // nsplit
    width = KVH_A * HD_A
    blk3 = lambda *s: pl.BlockSpec((1,) + s, lambda b, h, pt: (b, 0, 0))
    return pl.pallas_call(
        functools.partial(_s_attn_kernel, layer=layer, pages_per_step=pps, nsplit=nsplit),
        grid_spec=pltpu.PrefetchScalarGridSpec(
            num_scalar_prefetch=1,
            grid=(nb, nsplit),
            in_specs=[blk3(H_A, HD_A),
                      pl.BlockSpec((1, 1, 1, pps * PAGE), lambda b, h, pt: (b, h, 0, 0)),
                      blk3(1, LANES), blk3(1, width), blk3(1, width),
                      pl.BlockSpec(memory_space=pl.ANY), pl.BlockSpec(memory_space=pl.ANY)],
            out_specs=blk3(H_A, HD_A),
            scratch_shapes=[pltpu.VMEM((2, pps * PAGE, width), F32), pltpu.VMEM((2, pps * PAGE, width), F32),
                            pltpu.SemaphoreType.DMA((2,)), pltpu.SemaphoreType.DMA((2,)),
                            pltpu.VMEM((H_A, 1), F32), pltpu.VMEM((H_A, 1), F32), pltpu.VMEM((H_A, HD_A), F32)],
        ),
        out_shape=jax.ShapeDtypeStruct((nb, H_A, HD_A), F32),
        compiler_params=_params(("arbitrary", "arbitrary")),
        name="sample_attn",
    )(page_table, q, bias_past, bias_self, k_new, v_new, cache_k, cache_v)


def kernel(x_prompt, x_sample, cache_a_k, cache_a_v, cache_a_kidx, state_b_k, state_b_v, state_conv, page_table, norm_mix, norm_ffn, w_in_a, q_norm_a, k_norm_a, w_out_a, norm_kv_b, w_kv_b, k_norm_b, w_q_b, q_norm_b, sinks_b, w_out_b, w_up, conv_w, conv_b, w_down):
    bp, seq, _ = x_prompt.shape
    bs, tdec, _ = x_sample.shape
    assert tdec == 1
    n_a = w_in_a.shape[0]
    depth = w_up.shape[0]
    npages = page_table.shape[1]
    past = npages * PAGE
    n_pool = cache_a_k.shape[1]
    tm_p = min(512, seq)
    tm_mlp = min(1024, seq)
    tf = 256

    pos_p = jnp.arange(seq, dtype=I32)
    pos_s = jnp.full((bs,), past, I32)
    tabs = {}
    for name, pos in (("p", pos_p), ("s", pos_s)):
        tabs[name] = (_rope_tables(pos, HD_A), _rope_tables(pos, IDX_DIM),
                      _rope_tables(pos, IDX_DIM, pad_identity=LANES - IDX_DIM))

    row = lambda a: a.reshape(1, -1)
    tile2 = lambda a: jnp.tile(a.reshape(1, -1), (1, 2))
    w_in = jnp.pad(w_in_a, ((0, 0), (0, 0), (0, A_WIDTH_PAD - w_in_a.shape[2]))).astype(BF16)
    w_out_a_b = w_out_a.astype(BF16)
    w_up_b = w_up.astype(BF16)
    w_down_b = w_down.astype(BF16)
    w_kv_b_b = w_kv_b.astype(BF16)
    w_q_b_b = w_q_b.astype(BF16)
    w_out_b_b = w_out_b.astype(BF16)
    ck = cache_a_k.reshape(n_a, n_pool, PAGE, KVH_A * HD_A)
    cv = cache_a_v.reshape(n_a, n_pool, PAGE, KVH_A * HD_A)
    buf_k = state_b_k.reshape(bs, WINDOW, KVH_B * HD_B)
    buf_v = state_b_v.reshape(bs, WINDOW, KVH_B * HD_B)

    hp = x_prompt.reshape(bp * seq, D_MODEL)
    hs = x_sample.reshape(bs, D_MODEL)
    a_k_p, a_v_p, a_ki_p, a_k_s, a_v_s, a_ki_s, conv_p, conv_s = [], [], [], [], [], [], [], []
    topk_s = min(TOPK_MAX, (past + 1) // 4)

    for l in range(depth):
        if l < n_a:
            t128, t64, tki = tabs["p"]
            q, k, v, kb, vb, qi, misc, kib = _proj_a(hp, row(norm_mix[l]), w_in[l], row(q_norm_a[l]),
                                                     row(k_norm_a[l]), t128, t64, tki, tm_p)
            o = _dsa_prompt(q, qi, misc, kib, kb, vb, bp, seq)
            hp = _out_res(o, w_out_a_b[l], hp, tm_p)
            a_k_p.append(k.reshape(bp, seq, KVH_A, HD_A))
            a_v_p.append(v.reshape(bp, seq, KVH_A, HD_A))
            a_ki_p.append(misc[:, :IDX_DIM].reshape(bp, seq, IDX_DIM))

            t128, t64, tki = tabs["s"]
            q, k, v, kb, vb, qi, misc, kib = _proj_a(hs, row(norm_mix[l]), w_in[l], row(q_norm_a[l]),
                                                     row(k_norm_a[l]), t128, t64, tki, bs)
            wi = misc[:, IDX_DIM:IDX_DIM + IDX_HEADS].reshape(bs, IDX_HEADS, 1)
            sc = _s_scores(page_table, qi.reshape(bs, IDX_HEADS, IDX_DIM), wi, kib.reshape(bs, 1, IDX_DIM),
                           cache_a_kidx, l)
            bias = _s_select(sc.reshape(bs, past + LANES), past + 1, topk_s)
            nsplit = 2 if npages % 2 == 0 else 1
            o = _s_attn(page_table, q.astype(F32).reshape(bs, H_A, HD_A),
                        bias[:, :past].reshape(bs, nsplit, 1, past // nsplit),
                        bias[:, past:].reshape(bs, 1, LANES),
                        k.reshape(bs, 1, KVH_A * HD_A), v.reshape(bs, 1, KVH_A * HD_A), ck, cv, l)
            hs = _out_res(o.reshape(bs, H_A * HD_A), w_out_a_b[l], hs, bs)
            a_k_s.append(k.reshape(bs, 1, KVH_A, HD_A))
            a_v_s.append(v.reshape(bs, 1, KVH_A, HD_A))
            a_ki_s.append(misc[:, :IDX_DIM].reshape(bs, 1, IDX_DIM))
        else:
            b = l - n_a
            sink = jnp.broadcast_to(sinks_b[b][:, None], (H_B, LANES))
            qp = _q_b(hp, row(norm_mix[l]), w_q_b_b[b], tile2(q_norm_b[b]), tabs["p"][1], tm_p, BF16)
            o = _swa_prompt(qp, kb_p16, vb_p16, sink, bp, seq)
            hp = _out_res(o, w_out_b_b[b], hp, tm_p)
            qs = _q_b(hs, row(norm_mix[l]), w_q_b_b[b], tile2(q_norm_b[b]), tabs["s"][1], bs, F32)
            res = _swa_sample(qs.reshape(bs, H_B, HD_B), buf_k, buf_v, kb_s.reshape(bs, 1, KVH_B * HD_B),
                              vb_s.reshape(bs, 1, KVH_B * HD_B), sink, emit_state=(b == 0))
            if b == 0:
                new_b_k_s, new_b_v_s = res[1], res[2]
            hs = _out_res(res[0].reshape(bs, H_B * HD_B), w_out_b_b[b], hs, bs)

        hp, tail_p = _mlp_prompt(hp, row(norm_ffn[l]), w_up_b[l], conv_w[l], row(conv_b[l]), w_down_b[l],
                                 seq, tm_mlp, tf)
        conv_p.append(tail_p)
        hs, gate_s = _mlp_sample(hs, state_conv[l, :, 0, :], state_conv[l, :, 1, :], row(norm_ffn[l]), w_up_b[l],
                                 conv_w[l], row(conv_b[l]), w_down_b[l], tf)
        conv_s.append(jnp.stack([state_conv[l, :, 1, :], gate_s], axis=1))

        if l == n_a - 1:
            kb_p, vb_p, kb_p16, vb_p16 = _kv_b(hp, row(norm_kv_b), w_kv_b_b, tile2(k_norm_b), tabs["p"][1], tm_p)
            kb_s, vb_s, _, _ = _kv_b(hs, row(norm_kv_b), w_kv_b_b, tile2(k_norm_b), tabs["s"][1], bs)

    kb_p4 = kb_p.reshape(bp, seq, KVH_B, HD_B)
    vb_p4 = vb_p.reshape(bp, seq, KVH_B, HD_B)
    return (hp.reshape(bp, seq, D_MODEL), hs.reshape(bs, 1, D_MODEL),
            jnp.stack(a_k_p), jnp.stack(a_v_p), jnp.stack(a_ki_p),
            jnp.stack(a_k_s), jnp.stack(a_v_s), jnp.stack(a_ki_s),
            kb_p4[:, -WINDOW:], vb_p4[:, -WINDOW:],
            new_b_k_s.reshape(bs, WINDOW, KVH_B, HD_B), new_b_v_s.reshape(bs, WINDOW, KVH_B, HD_B),
            jnp.stack(conv_p), jnp.stack(conv_s))
```

```python
import functools
import math

import jax
import jax.numpy as jnp
from jax import lax
from jax.experimental import pallas as pl
from jax.experimental.pallas import tpu as pltpu

F32 = jnp.float32
BF16 = jnp.bfloat16
I32 = jnp.int32

D_MODEL = 1024
PAGE = 128
H_A, HD_A, KVH_A = 8, 128, 2
G_A = H_A // KVH_A
IDX_HEADS, IDX_DIM = 16, 64
TOPK_MAX = 256
QB = 128
H_B, HD_B, KVH_B = 16, 64, 2
G_B = H_B // KVH_B
WINDOW = 128
D_FF = 2816
ROPE_THETA = 500000.0
EPS = 1e-6
LANES = 128
SUBLANES = 8
A_Q0, A_K0, A_V0, A_QI0, A_KI0 = 0, 1024, 1280, 1536, 2560
A_WIDTH_PAD = 2688
INT_MIN = -(2 ** 31)
NEG = -1e30
LOG2E = math.log2(math.e)
VMEM_LIMIT = 56 * 1024 * 1024

_NT = (((1,), (1,)), ((), ()))


def _dot(a, b):
    return jnp.dot(a, b, preferred_element_type=F32)


def _dot_nt(a, b):
    return lax.dot_general(a, b, _NT, preferred_element_type=F32)


def _params(sem):
    return pltpu.CompilerParams(dimension_semantics=sem, vmem_limit_bytes=VMEM_LIMIT)


def _resident(shape):
    return pl.BlockSpec(shape, lambda *_: (0,) * len(shape), pipeline_mode=pl.Buffered(1))


def _rms(x, g):
    return x * lax.rsqrt(jnp.mean(x * x, axis=-1, keepdims=True) + EPS) * g


def _head_norm(x, g, hd):
    x2 = x * x
    if hd == LANES:
        r = lax.rsqrt(jnp.sum(x2, axis=-1, keepdims=True) * (1.0 / hd) + EPS)
    else:
        lane = lax.broadcasted_iota(I32, x.shape, 1)
        lo = lane < hd
        s_lo = jnp.sum(jnp.where(lo, x2, 0.0), axis=-1, keepdims=True)
        s_hi = jnp.sum(jnp.where(lo, 0.0, x2), axis=-1, keepdims=True)
        r = jnp.where(lo, lax.rsqrt(s_lo * (1.0 / hd) + EPS), lax.rsqrt(s_hi * (1.0 / hd) + EPS))
    return x * r * g


def _rope(x, tab_ref, half):
    c, sa, sb = tab_ref[0], tab_ref[1], tab_ref[2]
    return x * c + pltpu.roll(x, LANES - half, 1) * sa + pltpu.roll(x, half, 1) * sb


def _rope_tables(pos, hd, pad_identity=0):
    rot = hd // 4
    half = rot // 2
    inv = ROPE_THETA ** (-jnp.arange(half, dtype=F32) / half)
    ang = pos.astype(F32)[:, None] * inv[None, :]
    cos, sin = jnp.cos(ang), jnp.sin(ang)
    t = pos.shape[0]
    ones = lambda n: jnp.ones((t, n), F32)
    zeros = lambda n: jnp.zeros((t, n), F32)
    c = jnp.concatenate([cos, cos, ones(hd - rot)], axis=1)
    sa = jnp.concatenate([-sin, zeros(hd - half)], axis=1)
    sb = jnp.concatenate([zeros(half), sin, zeros(hd - rot)], axis=1)
    if pad_identity:
        c = jnp.concatenate([c, ones(pad_identity)], axis=1)
        sa = jnp.concatenate([sa, zeros(pad_identity)], axis=1)
        sb = jnp.concatenate([sb, zeros(pad_identity)], axis=1)
    reps = LANES // c.shape[1]
    return jnp.stack([jnp.tile(a, (1, reps)) for a in (c, sa, sb)])


def _sort_key(x):
    bits = lax.bitcast_convert_type(x, I32)
    return jnp.where(bits >= 0, bits, bits ^ 0x7FFFFFFF)


def _radix_descend(count_ge, lo, nbits, k):
    def bit_step(i, lo):
        cand = lo ^ jnp.left_shift(jnp.int32(1), nbits - 1 - i)
        return jnp.where(count_ge(cand) >= k, cand, lo)

    return lax.fori_loop(0, nbits, bit_step, lo)


def _fold_lanes(m):
    part = m[:, 0:LANES]
    for t in range(1, m.shape[1] // LANES):
        part = part + m[:, t * LANES:(t + 1) * LANES]
    return part


def _fold_rows(x, op):
    vs = [x[i * SUBLANES:(i + 1) * SUBLANES] for i in range(x.shape[0] // SUBLANES)]
    while len(vs) > 1:
        vs = [op(vs[i], vs[i + 1]) if i + 1 < len(vs) else vs[i] for i in range(0, len(vs), 2)]
    return vs[0]


def _split_pair(blk):
    x = blk.astype(F32)
    lane = lax.broadcasted_iota(I32, x.shape, 1)
    lo = lane < LANES // 2
    return jnp.concatenate([jnp.where(lo, x, 0.0), jnp.where(lo, 0.0, x)], axis=0).astype(BF16)


def _proj_a_kernel(x_ref, g_ref, w_ref, qn_ref, kn_ref, t128_ref, t64_ref, tki_ref,
                   q_ref, k_ref, v_ref, kb_ref, vt_ref, qi_ref, misc_ref, ki2_ref):
    xn = _rms(x_ref[...], g_ref[...]).astype(BF16)
    y = _dot(xn, w_ref[...])
    qn, kn = qn_ref[...], kn_ref[...]
    for h in range(H_A):
        xh = _head_norm(y[:, A_Q0 + h * HD_A:A_Q0 + (h + 1) * HD_A], qn, HD_A)
        q_ref[:, h * HD_A:(h + 1) * HD_A] = _rope(xh, t128_ref, HD_A // 8).astype(BF16)
    for n in range(KVH_A):
        xh = _head_norm(y[:, A_K0 + n * HD_A:A_K0 + (n + 1) * HD_A], kn, HD_A)
        kh = _rope(xh, t128_ref, HD_A // 8)
        k_ref[:, n * HD_A:(n + 1) * HD_A] = kh
        kb_ref[:, n * HD_A:(n + 1) * HD_A] = kh.astype(BF16)
    v = y[:, A_V0:A_QI0]
    v_ref[...] = v
    vt_ref[0] = jnp.transpose(v).astype(BF16)
    for c in range(IDX_HEADS * IDX_DIM // LANES):
        xh = y[:, A_QI0 + c * LANES:A_QI0 + (c + 1) * LANES]
        qi_ref[:, c * LANES:(c + 1) * LANES] = _rope(xh, t64_ref, IDX_DIM // 8).astype(BF16)
    m = _rope(y[:, A_KI0:A_WIDTH_PAD], tki_ref, IDX_DIM // 8)
    misc_ref[...] = m
    lane = lax.broadcasted_iota(I32, m.shape, 1)
    ki2_ref[...] = jnp.where(lane < IDX_DIM, m, pltpu.roll(m, IDX_DIM, 1)).astype(BF16)


def _proj_a(h, g, w, qn, kn, t128, t64, tki, tm):
    m = h.shape[0]
    nt = t128.shape[1] // tm
    row = lambda i: (i, 0)
    const = lambda i: (0, 0)
    tab = lambda i: (0, i % nt, 0)
    kvw = KVH_A * HD_A
    out_shapes = (
        jax.ShapeDtypeStruct((m, H_A * HD_A), BF16),
        jax.ShapeDtypeStruct((m, kvw), F32),
        jax.ShapeDtypeStruct((m, kvw), F32),
        jax.ShapeDtypeStruct((m, kvw), BF16),
        jax.ShapeDtypeStruct((m // tm, kvw, tm), BF16),
        jax.ShapeDtypeStruct((m, IDX_HEADS * IDX_DIM), BF16),
        jax.ShapeDtypeStruct((m, LANES), F32),
        jax.ShapeDtypeStruct((m, LANES), BF16),
    )
    out_specs = [pl.BlockSpec((tm, s.shape[1]), row) if len(s.shape) == 2
                 else pl.BlockSpec((1, kvw, tm), lambda i: (i, 0, 0)) for s in out_shapes]
    return pl.pallas_call(
        _proj_a_kernel,
        grid=(m // tm,),
        in_specs=[
            pl.BlockSpec((tm, D_MODEL), row),
            pl.BlockSpec((1, D_MODEL), const),
            _resident((D_MODEL, A_WIDTH_PAD)),
            pl.BlockSpec((1, LANES), const),
            pl.BlockSpec((1, LANES), const),
            pl.BlockSpec((3, tm, LANES), tab),
            pl.BlockSpec((3, tm, LANES), tab),
            pl.BlockSpec((3, tm, LANES), tab),
        ],
        out_specs=out_specs,
        out_shape=out_shapes,
        compiler_params=_params(("parallel",)),
        name="proj_a",
    )(h, g, w, qn, kn, t128, t64, tki)


def _dsa_prompt_kernel(q_ref, qi_ref, w_ref, ki2_ref, kb_ref, vt_ref, o_ref, key_ref, bias_ref, t_ref, *, ch, topk):
    j = pl.program_id(1)
    t0 = j * QB
    nch = (t0 + QB + ch - 1) // ch
    wt = jnp.transpose(w_ref[...])[IDX_DIM:IDX_DIM + IDX_HEADS, :] * (IDX_DIM ** -0.5 * IDX_HEADS ** -0.5)
    wrows = [wt[h:h + 1, :] for h in range(IDX_HEADS)]
    npair = IDX_HEADS // 2
    pairs = [_split_pair(qi_ref[:, c * LANES:(c + 1) * LANES]) for c in range(npair)]
    krow = lax.broadcasted_iota(I32, (ch, QB), 0)
    qcol = lax.broadcasted_iota(I32, (ch, QB), 1) + t0

    def score_chunk(c, carry):
        ks = pl.multiple_of(c * ch, ch)
        ki2 = ki2_ref[pl.ds(ks, ch), :]
        acc = jnp.zeros((ch, QB), F32)
        for cc in range(npair):
            d = _dot_nt(ki2, pairs[cc])
            acc = acc + jnp.maximum(d[:, :QB], 0.0) * wrows[2 * cc] + jnp.maximum(d[:, QB:], 0.0) * wrows[2 * cc + 1]
        key_ref[c] = jnp.where(krow + ks <= qcol, _sort_key(acc), INT_MIN)
        return carry

    lax.fori_loop(0, nch, score_chunk, 0)

    def count_ge(cand):
        def body(c, cnt):
            return cnt + _fold_rows(jnp.where(key_ref[c] >= cand, 1.0, 0.0), jnp.add)

        cnt = lax.fori_loop(0, nch, body, jnp.zeros((SUBLANES, QB), F32))
        return jnp.sum(cnt, axis=0, keepdims=True)

    if ch % TOPK_MAX == 0:
        def group_max(c, gmax):
            kk = key_ref[c]
            for s in range(ch // TOPK_MAX):
                gmax = jnp.maximum(gmax, kk[s * TOPK_MAX:(s + 1) * TOPK_MAX])
            return gmax

        gmax = lax.fori_loop(0, nch, group_max, jnp.full((TOPK_MAX, QB), INT_MIN, I32))
        upper = jnp.max(gmax, axis=0, keepdims=True)
        lower = jnp.min(gmax, axis=0, keepdims=True)
        nbits = jnp.max(32 - lax.clz(lower ^ upper))
        keep = jnp.where(nbits >= 32, 0, jnp.left_shift(jnp.int32(-1), jnp.minimum(nbits, 31)))
        lo0 = ((lower ^ INT_MIN) & keep) ^ INT_MIN
    else:
        nbits = 32
        lo0 = jnp.full((1, QB), INT_MIN, I32)

    thr = jnp.maximum(_radix_descend(count_ge, lo0, nbits, topk), INT_MIN + 1)

    def bias_chunk(c, carry):
        bias_ref[c] = jnp.where(key_ref[c] >= thr, 0.0, NEG)
        return carry

    lax.fori_loop(0, nch, bias_chunk, 0)

    cfac = HD_A ** -0.5 * LOG2E
    qns = [jnp.concatenate([q_ref[:, (n * G_A + g) * HD_A:(n * G_A + g + 1) * HD_A] for g in range(G_A)], axis=0)
           for n in range(KVH_A)]

    def logits_chunk(c, mxs):
        ks = pl.multiple_of(c * ch, ch)
        b4 = jnp.concatenate([bias_ref[c]] * G_A, axis=1)
        out = []
        for n in range(KVH_A):
            kc = kb_ref[pl.ds(ks, ch), n * HD_A:(n + 1) * HD_A]
            t = _dot_nt(kc, qns[n]) * cfac + b4
            t_ref[n, c] = t
            out.append(jnp.maximum(mxs[n], _fold_rows(t, jnp.maximum)))
        return tuple(out)

    mxs = lax.fori_loop(0, nch, logits_chunk, (jnp.full((SUBLANES, G_A * QB), NEG, F32),) * KVH_A)
    ms = [jnp.max(mx, axis=0, keepdims=True) for mx in mxs]

    def pv_chunk(c, carry):
        out = []
        for n in range(KVH_A):
            lsum, acc = carry[n]
            p = jnp.exp2(t_ref[n, c] - ms[n])
            acc = acc + _dot(vt_ref[c, n * HD_A:(n + 1) * HD_A, :], p.astype(BF16))
            out.append((lsum + _fold_rows(p, jnp.add), acc))
        return tuple(out)

    init = ((jnp.zeros((SUBLANES, G_A * QB), F32), jnp.zeros((HD_A, G_A * QB), F32)),) * KVH_A
    res = lax.fori_loop(0, nch, pv_chunk, init)
    for n in range(KVH_A):
        lsum, acc = res[n]
        out = acc / jnp.sum(lsum, axis=0, keepdims=True)
        for g in range(G_A):
            h = n * G_A + g
            o_ref[:, h * HD_A:(h + 1) * HD_A] = jnp.transpose(out[:, g * QB:(g + 1) * QB]).astype(BF16)


def _dsa_prompt(q, qi, misc, ki2, kb, vt, nbatch, seq, ch):
    nq = seq // QB
    nchunks = seq // ch
    topk = min(TOPK_MAX, seq // 4)
    kvw = KVH_A * HD_A
    qblk = lambda b, j: (b * nq + j, 0)
    bat = lambda b, j: (b, 0)
    return pl.pallas_call(
        functools.partial(_dsa_prompt_kernel, ch=ch, topk=topk),
        grid=(nbatch, nq),
        in_specs=[
            pl.BlockSpec((QB, H_A * HD_A), qblk),
            pl.BlockSpec((QB, IDX_HEADS * IDX_DIM), qblk),
            pl.BlockSpec((QB, LANES), qblk),
            pl.BlockSpec((seq, LANES), bat),
            pl.BlockSpec((seq, kvw), bat),
            pl.BlockSpec((nchunks, kvw, ch), lambda b, j: (b, 0, 0)),
        ],
        out_specs=pl.BlockSpec((QB, H_A * HD_A), qblk),
        out_shape=jax.ShapeDtypeStruct((nbatch * seq, H_A * HD_A), BF16),
        scratch_shapes=[pltpu.VMEM((nchunks, ch, QB), I32), pltpu.VMEM((nchunks, ch, QB), F32),
                        pltpu.VMEM((KVH_A, nchunks, ch, G_A * QB), F32)],
        compiler_params=_params(("parallel", "parallel")),
        name="dsa_prompt",
    )(q, qi, misc, ki2, kb, vt)


def _out_res_kernel(o_ref, w_ref, h_ref, y_ref):
    y_ref[...] = h_ref[...] + _dot(o_ref[...].astype(BF16), w_ref[...])


def _out_res(o, w, h, tm):
    m, kdim = o.shape
    row = lambda i: (i, 0)
    return pl.pallas_call(
        _out_res_kernel,
        grid=(m // tm,),
        in_specs=[pl.BlockSpec((tm, kdim), row), _resident((kdim, D_MODEL)),
                  pl.BlockSpec((tm, D_MODEL), row)],
        out_specs=pl.BlockSpec((tm, D_MODEL), row),
        out_shape=jax.ShapeDtypeStruct((m, D_MODEL), F32),
        compiler_params=_params(("parallel",)),
        name="out_res",
    )(o, w, h)


HALO = 16


def _silu(x):
    return x * jax.nn.sigmoid(x)


def _mlp_prompt_kernel(h_ref, halo_ref, g_ref, wup_ref, cw_ref, cb_ref, wd_ref,
                       out_ref, tail_ref, xn_ref, gs_ref, act_ref, *, tm, tf, tiles_per_seq):
    i = pl.program_id(0)
    g = g_ref[...]
    xn_ref[0:HALO, :] = _rms(halo_ref[...], g).astype(BF16)
    xn_ref[HALO:HALO + tm, :] = _rms(h_ref[...], g).astype(BF16)
    first = i % tiles_per_seq == 0
    for f in range(D_FF // tf):
        cols = slice(f * tf, (f + 1) * tf)
        gs_ref[...] = _dot(xn_ref[...], wup_ref[:, cols])
        up = _dot(xn_ref[HALO:HALO + tm, :], wup_ref[:, D_FF + f * tf:D_FF + (f + 1) * tf])

        @pl.when(first)
        def _():
            gs_ref[0:HALO, :] = jnp.zeros((HALO, tf), F32)

        conv = (cb_ref[:, cols] + cw_ref[0:1, cols] * gs_ref[HALO - 2:HALO - 2 + tm, :]
                + cw_ref[1:2, cols] * gs_ref[HALO - 1:HALO - 1 + tm, :]
                + cw_ref[2:3, cols] * gs_ref[HALO:HALO + tm, :])
        act_ref[:, cols] = (_silu(conv) * up).astype(BF16)
        tail_ref[0, :, cols] = gs_ref[tm:tm + HALO, :]
    out_ref[...] = h_ref[...] + _dot(act_ref[...], wd_ref[...])


def _mlp_prompt(h, g, w_up, cw, cb, w_down, seq, tm, tf):
    m = h.shape[0]
    tiles_per_seq = seq // tm
    hb = tm // HALO
    out, tails = pl.pallas_call(
        functools.partial(_mlp_prompt_kernel, tm=tm, tf=tf, tiles_per_seq=tiles_per_seq),
        grid=(m // tm,),
        in_specs=[
            pl.BlockSpec((tm, D_MODEL), lambda i: (i, 0)),
            pl.BlockSpec((HALO, D_MODEL), lambda i: (jnp.maximum(i * hb - 1, 0), 0)),
            _resident((1, D_MODEL)),
            _resident((D_MODEL, 2 * D_FF)),
            _resident((3, D_FF)),
            _resident((1, D_FF)),
            _resident((D_FF, D_MODEL)),
        ],
        out_specs=[pl.BlockSpec((tm, D_MODEL), lambda i: (i, 0)),
                   pl.BlockSpec((1, HALO, D_FF), lambda i: (i, 0, 0))],
        out_shape=(jax.ShapeDtypeStruct((m, D_MODEL), F32),
                   jax.ShapeDtypeStruct((m // tm, HALO, D_FF), F32)),
        scratch_shapes=[pltpu.VMEM((tm + HALO, D_MODEL), BF16), pltpu.VMEM((tm + HALO, tf), F32),
                        pltpu.VMEM((tm, D_FF), BF16)],
        compiler_params=_params(("parallel",)),
        name="mlp_prompt",
    )(h, h, g, w_up, cw, cb, w_down)
    nb = m // seq
    tail = tails.reshape(nb, tiles_per_seq, HALO, D_FF)[:, -1, HALO - 2:, :]
    return out, tail


def _mlp_sample_kernel(h_ref, s0_ref, s1_ref, g_ref, wg_ref, wu_ref, cw_ref, cb_ref, wd_ref,
                       out_ref, gate_ref, xn_ref, acc_ref):
    j = pl.program_id(0)

    @pl.when(j == 0)
    def _():
        xn_ref[...] = _rms(h_ref[...], g_ref[...]).astype(BF16)
        acc_ref[...] = jnp.zeros_like(acc_ref)

    xn = xn_ref[...]
    gate = _dot(xn, wg_ref[...])
    up = _dot(xn, wu_ref[...])
    gate_ref[...] = gate
    conv = cb_ref[...] + cw_ref[0:1, :] * s0_ref[...] + cw_ref[1:2, :] * s1_ref[...] + cw_ref[2:3, :] * gate
    act = (_silu(conv) * up).astype(BF16)
    acc_ref[...] += _dot(act, wd_ref[...])

    @pl.when(j == pl.num_programs(0) - 1)
    def _():
        out_ref[...] = h_ref[...] + acc_ref[...]


def _mlp_sample(h, s0, s1, g, w_up, cw, cb, w_down, tf):
    m = h.shape[0]
    nf = D_FF // tf
    full = lambda j: (0, 0)
    col = lambda j: (0, j)
    return pl.pallas_call(
        _mlp_sample_kernel,
        grid=(nf,),
        in_specs=[
            pl.BlockSpec((m, D_MODEL), full),
            pl.BlockSpec((m, tf), col),
            pl.BlockSpec((m, tf), col),
            pl.BlockSpec((1, D_MODEL), full),
            pl.BlockSpec((D_MODEL, tf), col),
            pl.BlockSpec((D_MODEL, tf), lambda j: (0, nf + j)),
            pl.BlockSpec((3, tf), col),
            pl.BlockSpec((1, tf), col),
            pl.BlockSpec((tf, D_MODEL), lambda j: (j, 0)),
        ],
        out_specs=[pl.BlockSpec((m, D_MODEL), full), pl.BlockSpec((m, tf), col)],
        out_shape=(jax.ShapeDtypeStruct((m, D_MODEL), F32), jax.ShapeDtypeStruct((m, D_FF), F32)),
        scratch_shapes=[pltpu.VMEM((m, D_MODEL), BF16), pltpu.VMEM((m, D_MODEL), F32)],
        compiler_params=_params(("arbitrary",)),
        name="mlp_sample",
    )(h, s0, s1, g, w_up, w_up, cw, cb, w_down)


def _kv_b_kernel(x_ref, g_ref, w_ref, kn_ref, t64_ref, k_ref, v_ref, kdup_ref, vt_ref):
    xn = _rms(x_ref[...], g_ref[...]).astype(BF16)
    y = _dot(xn, w_ref[...])
    k = _rope(_head_norm(y[:, :LANES], kn_ref[...], HD_B), t64_ref, HD_B // 8)
    v = y[:, LANES:]
    k_ref[...] = k
    v_ref[...] = v
    lane = lax.broadcasted_iota(I32, k.shape, 1)
    kr = pltpu.roll(k, HD_B, 1)
    kdup_ref[:, :LANES] = jnp.where(lane < HD_B, k, kr).astype(BF16)
    kdup_ref[:, LANES:] = jnp.where(lane < HD_B, kr, k).astype(BF16)
    for s in range(vt_ref.shape[0]):
        vt_ref[s] = jnp.transpose(v[s * WINDOW:(s + 1) * WINDOW, :]).astype(BF16)


def _kv_b(h, g, w, kn, t64, tm):
    m = h.shape[0]
    nt = t64.shape[1] // tm
    row = lambda i: (i, 0)
    const = lambda i: (0, 0)
    width = KVH_B * HD_B
    nsub = max(tm // WINDOW, 1)
    vt_rows = min(tm, WINDOW)
    shapes = (jax.ShapeDtypeStruct((m, width), F32), jax.ShapeDtypeStruct((m, width), F32),
              jax.ShapeDtypeStruct((m, 2 * width), BF16),
              jax.ShapeDtypeStruct((m // vt_rows, width, vt_rows), BF16))
    return pl.pallas_call(
        _kv_b_kernel,
        grid=(m // tm,),
        in_specs=[pl.BlockSpec((tm, D_MODEL), row), pl.BlockSpec((1, D_MODEL), const),
                  _resident((D_MODEL, 2 * width)), pl.BlockSpec((1, LANES), const),
                  pl.BlockSpec((3, tm, LANES), lambda i: (0, i % nt, 0))],
        out_specs=[pl.BlockSpec((tm, width), row), pl.BlockSpec((tm, width), row),
                   pl.BlockSpec((tm, 2 * width), row),
                   pl.BlockSpec((nsub, width, vt_rows), lambda i: (i, 0, 0))],
        out_shape=shapes,
        compiler_params=_params(("parallel",)),
        name="kv_b",
    )(h, g, w, kn, t64)


def _q_b_kernel(x_ref, g_ref, w_ref, qn_ref, t64_ref, q_ref):
    xn = _rms(x_ref[...], g_ref[...]).astype(BF16)
    y = _dot(xn, w_ref[...])
    qn = qn_ref[...]
    for c in range(H_B * HD_B // LANES):
        xh = _head_norm(y[:, c * LANES:(c + 1) * LANES], qn, HD_B)
        q_ref[:, c * LANES:(c + 1) * LANES] = (_rope(xh, t64_ref, HD_B // 8) * (HD_B ** -0.5)).astype(q_ref.dtype)


def _q_b(h, g, w, qn, t64, tm, dtype):
    m = h.shape[0]
    nt = t64.shape[1] // tm
    row = lambda i: (i, 0)
    const = lambda i: (0, 0)
    return pl.pallas_call(
        _q_b_kernel,
        grid=(m // tm,),
        in_specs=[pl.BlockSpec((tm, D_MODEL), row), pl.BlockSpec((1, D_MODEL), const),
                  _resident((D_MODEL, H_B * HD_B)), pl.BlockSpec((1, LANES), const),
                  pl.BlockSpec((3, tm, LANES), lambda i: (0, i % nt, 0))],
        out_specs=pl.BlockSpec((tm, H_B * HD_B), row),
        out_shape=jax.ShapeDtypeStruct((m, H_B * HD_B), dtype),
        compiler_params=_params(("parallel",)),
        name="q_b",
    )(h, g, w, qn, t64)


def _swa_prompt_kernel(q_ref, kc_ref, kp_ref, vtc_ref, vtp_ref, sink_ref, o_ref):
    j = pl.program_id(1)
    kr = lax.broadcasted_iota(I32, (2 * WINDOW, WINDOW), 0)
    qc = lax.broadcasted_iota(I32, (2 * WINDOW, WINDOW), 1)
    rel = kr - WINDOW - qc
    ok = (rel <= 0) & (rel >= -WINDOW) & ((kr >= WINDOW) | (j > 0))
    bias = jnp.where(ok, 0.0, NEG)
    bias2 = jnp.concatenate([bias, bias], axis=1)
    kband = jnp.concatenate([kp_ref[...], kc_ref[...]], axis=0)
    vt = jnp.concatenate([vtp_ref[0], vtc_ref[0]], axis=1)
    for c in range(H_B // 2):
        n = (2 * c) // G_B
        s = _dot_nt(kband[:, n * LANES:(n + 1) * LANES], _split_pair(q_ref[:, c * LANES:(c + 1) * LANES])) + bias2
        sink = sink_ref[c:c + 1, :]
        m = jnp.maximum(jnp.max(s, axis=0, keepdims=True), sink)
        p = jnp.exp(s - m)
        denom = jnp.sum(p, axis=0, keepdims=True) + jnp.exp(sink - m)
        o_t = _dot(vt[n * HD_B:(n + 1) * HD_B, :], p.astype(BF16)) / denom
        stacked = jnp.concatenate([o_t[:, :WINDOW], o_t[:, WINDOW:]], axis=0)
        o_ref[:, c * LANES:(c + 1) * LANES] = jnp.transpose(stacked).astype(BF16)


def _swa_prompt(q, kdup, vt, sink2, nbatch, seq):
    nblk = seq // WINDOW
    cur = lambda b, j: (b * nblk + j, 0)
    prev = lambda b, j: (b * nblk + jnp.maximum(j - 1, 0), 0)
    cur3 = lambda b, j: (b * nblk + j, 0, 0)
    prev3 = lambda b, j: (b * nblk + jnp.maximum(j - 1, 0), 0, 0)
    width = KVH_B * HD_B
    return pl.pallas_call(
        _swa_prompt_kernel,
        grid=(nbatch, nblk),
        in_specs=[pl.BlockSpec((WINDOW, H_B * HD_B), cur),
                  pl.BlockSpec((WINDOW, 2 * width), cur), pl.BlockSpec((WINDOW, 2 * width), prev),
                  pl.BlockSpec((1, width, WINDOW), cur3), pl.BlockSpec((1, width, WINDOW), prev3),
                  pl.BlockSpec((H_B // 2, 2 * WINDOW), lambda b, j: (0, 0))],
        out_specs=pl.BlockSpec((WINDOW, H_B * HD_B), cur),
        out_shape=jax.ShapeDtypeStruct((nbatch * seq, H_B * HD_B), BF16),
        compiler_params=_params(("parallel", "parallel")),
        name="swa_prompt",
    )(q, kdup, kdup, vt, vt, sink2)


def _swa_sample_kernel(q_ref, bk_ref, bv_ref, kn_ref, vn_ref, sink_ref, o_ref, *state_refs):
    q = q_ref[0]
    bk, bv = bk_ref[0], bv_ref[0]
    knew, vnew = kn_ref[0], vn_ref[0]
    rowh = lax.broadcasted_iota(I32, (H_B, 1), 0)
    first = rowh < G_B
    qb = q.astype(BF16)
    s_past = jnp.where(first, _dot_nt(qb, bk[:, :HD_B].astype(BF16)), _dot_nt(qb, bk[:, HD_B:].astype(BF16)))
    knew_h = jnp.where(first, knew[:, :HD_B], knew[:, HD_B:])
    vnew_h = jnp.where(first, vnew[:, :HD_B], vnew[:, HD_B:])
    s_new = jnp.sum(q * knew_h, axis=-1, keepdims=True)
    sink = sink_ref[:, 0:1]
    m = jnp.maximum(jnp.maximum(jnp.max(s_past, axis=-1, keepdims=True), s_new), sink)
    p_past = jnp.exp(s_past - m)
    p_new = jnp.exp(s_new - m)
    denom = jnp.sum(p_past, axis=-1, keepdims=True) + p_new + jnp.exp(sink - m)
    pb = (p_past / denom).astype(BF16)
    o_past = jnp.where(first, _dot(pb, bv[:, :HD_B].astype(BF16)), _dot(pb, bv[:, HD_B:].astype(BF16)))
    o_ref[0] = o_past + (p_new / denom) * vnew_h
    if state_refs:
        nk_ref, nv_ref = state_refs
        nk_ref[0] = jnp.concatenate([bk[1:], knew], axis=0)
        nv_ref[0] = jnp.concatenate([bv[1:], vnew], axis=0)


def _swa_sample(q, buf_k, buf_v, k_new, v_new, sinks, emit_state):
    nb = q.shape[0]
    width = KVH_B * HD_B
    blk3 = lambda *s: pl.BlockSpec((1,) + s, lambda b: (b, 0, 0))
    out_shape = [jax.ShapeDtypeStruct((nb, H_B, HD_B), F32)]
    out_specs = [blk3(H_B, HD_B)]
    if emit_state:
        out_shape += [jax.ShapeDtypeStruct((nb, WINDOW, width), F32)] * 2
        out_specs += [blk3(WINDOW, width)] * 2
    return pl.pallas_call(
        _swa_sample_kernel,
        grid=(nb,),
        in_specs=[blk3(H_B, HD_B), blk3(WINDOW, width), blk3(WINDOW, width), blk3(1, width), blk3(1, width),
                  pl.BlockSpec((H_B, LANES), lambda b: (0, 0))],
        out_specs=out_specs,
        out_shape=out_shape,
        compiler_params=_params(("parallel",)),
        name="swa_sample",
    )(q, buf_k, buf_v, k_new, v_new, sinks)


def _s_scores_kernel(pt_ref, qi_ref, w_ref, kin_ref, cache_ref, out_ref, buf_ref, sem, *, layer, npages):
    b = pl.program_id(0)
    nb = pl.num_programs(0)
    slot = b % 2

    def copies(bb, sl):
        return [pltpu.make_async_copy(cache_ref.at[layer, pt_ref[bb, p]],
                                      buf_ref.at[sl, :, pl.ds(p * PAGE, PAGE)], sem.at[sl])
                for p in range(npages)]

    @pl.when(b == 0)
    def _():
        for cp in copies(0, 0):
            cp.start()

    @pl.when(b + 1 < nb)
    def _():
        for cp in copies(b + 1, 1 - slot):
            cp.start()

    for cp in copies(b, slot):
        cp.wait()

    qi = qi_ref[0]
    w = w_ref[0] * (IDX_DIM ** -0.5 * IDX_HEADS ** -0.5)
    past = npages * PAGE
    d_past = _dot(qi, buf_ref[slot].astype(BF16))
    d_new = _dot_nt(qi, jnp.broadcast_to(kin_ref[0], (LANES, IDX_DIM)))
    out_ref[0, :, 0:past] = jnp.sum(jnp.maximum(d_past, 0.0) * w, axis=0, keepdims=True)
    out_ref[0, :, past:past + LANES] = jnp.sum(jnp.maximum(d_new, 0.0) * w, axis=0, keepdims=True)


def _s_scores(page_table, qi, wi, ki_new, cache_t, layer):
    nb, npages = page_table.shape
    past = npages * PAGE
    blk3 = lambda *s: pl.BlockSpec((1,) + s, lambda b, pt: (b, 0, 0))
    return pl.pallas_call(
        functools.partial(_s_scores_kernel, layer=layer, npages=npages),
        grid_spec=pltpu.PrefetchScalarGridSpec(
            num_scalar_prefetch=1,
            grid=(nb,),
            in_specs=[blk3(IDX_HEADS, IDX_DIM), blk3(IDX_HEADS, 1), blk3(1, IDX_DIM),
                      pl.BlockSpec(memory_space=pl.ANY)],
            out_specs=blk3(1, past + LANES),
            scratch_shapes=[pltpu.VMEM((2, IDX_DIM, past), F32), pltpu.SemaphoreType.DMA((2,))],
        ),
        out_shape=jax.ShapeDtypeStruct((nb, 1, past + LANES), F32),
        compiler_params=_params(("arbitrary",)),
        name="sample_scores",
    )(page_table, qi, wi, ki_new, cache_t)


def _s_select_kernel(sc_ref, bias_ref, key_ref, *, nvalid, topk, ch):
    rows, width = sc_ref.shape
    nch = width // ch
    for c in range(nch):
        lane = lax.broadcasted_iota(I32, (rows, ch), 1) + c * ch
        key_ref[:, c * ch:(c + 1) * ch] = jnp.where(lane < nvalid, _sort_key(sc_ref[:, c * ch:(c + 1) * ch]), INT_MIN)

    def count_ge(cand):
        cnt = jnp.zeros((rows, LANES), F32)
        for c in range(nch):
            cnt = cnt + _fold_lanes(jnp.where(key_ref[:, c * ch:(c + 1) * ch] >= cand, 1.0, 0.0))
        return jnp.sum(cnt, axis=1, keepdims=True)

    thr = _radix_descend(count_ge, jnp.full((rows, 1), INT_MIN, I32), 32, topk)
    thr = jnp.maximum(thr, INT_MIN + 1)
    for c in range(nch):
        bias_ref[:, c * ch:(c + 1) * ch] = jnp.where(key_ref[:, c * ch:(c + 1) * ch] >= thr, 0.0, NEG)


def _s_select(sc, nvalid, topk):
    rows, width = sc.shape
    ch = LANES * 5 if width % (LANES * 5) == 0 else LANES
    return pl.pallas_call(
        functools.partial(_s_select_kernel, nvalid=nvalid, topk=topk, ch=ch),
        out_shape=jax.ShapeDtypeStruct((rows, width), F32),
        scratch_shapes=[pltpu.VMEM((rows, width), I32)],
        compiler_params=pltpu.CompilerParams(vmem_limit_bytes=VMEM_LIMIT),
        name="sample_select",
    )(sc)


def _s_attn_kernel(pt_ref, q_ref, bias_ref, bself_ref, kn_ref, vn_ref, kc_ref, vc_ref, o_ref,
                   kbuf, vbuf, ksem, vsem, m_ref, l_ref, acc_ref, *, layer, pages_per_step, nsplit):
    b = pl.program_id(0)
    hf = pl.program_id(1)
    step = b * nsplit + hf
    nsteps = pl.num_programs(0) * nsplit
    slot = step % 2
    prow = PAGE * KVH_A

    def copies(st, sl):
        bb = st // nsplit
        p0 = (st % nsplit) * pages_per_step
        out = []
        for cache, buf, sem in ((kc_ref, kbuf, ksem), (vc_ref, vbuf, vsem)):
            out += [pltpu.make_async_copy(cache.at[layer, pt_ref[bb, p0 + p]],
                                          buf.at[sl, pl.ds(p * prow, prow)], sem.at[sl])
                    for p in range(pages_per_step)]
        return out

    @pl.when(step == 0)
    def _():
        for cp in copies(0, 0):
            cp.start()

    @pl.when(step + 1 < nsteps)
    def _():
        for cp in copies(step + 1, 1 - slot):
            cp.start()

    for cp in copies(step, slot):
        cp.wait()

    @pl.when(hf == 0)
    def _():
        m_ref[...] = jnp.full(m_ref.shape, NEG, F32)
        l_ref[...] = jnp.zeros_like(l_ref)
        acc_ref[...] = jnp.zeros_like(acc_ref)

    scale = HD_A ** -0.5
    q = q_ref[0]
    nrow = pages_per_step * prow
    head_kv = jnp.right_shift(lax.broadcasted_iota(I32, (H_A, nrow), 0), G_A.bit_length() - 1)
    col_kv = jnp.bitwise_and(lax.broadcasted_iota(I32, (H_A, nrow), 1), KVH_A - 1)
    s = _dot_nt(q.astype(BF16), kbuf[slot].astype(BF16)) * scale + bias_ref[0, 0]
    s = jnp.where(head_kv == col_kv, s, NEG)
    m = m_ref[...]
    m_new = jnp.maximum(m, jnp.max(s, axis=-1, keepdims=True))
    p = jnp.exp(s - m_new)
    alpha = jnp.exp(m - m_new)
    l_ref[...] = alpha * l_ref[...] + jnp.sum(p, axis=-1, keepdims=True)
    acc_ref[...] = alpha * acc_ref[...] + _dot(p.astype(BF16), vbuf[slot].astype(BF16))
    m_ref[...] = m_new

    @pl.when(hf == nsplit - 1)
    def _():
        first = lax.broadcasted_iota(I32, (H_A, 1), 0) < G_A
        knew, vnew = kn_ref[0], vn_ref[0]
        knew_h = jnp.where(first, knew[:, :HD_A], knew[:, HD_A:])
        vnew_h = jnp.where(first, vnew[:, :HD_A], vnew[:, HD_A:])
        s_new = jnp.sum(q * knew_h, axis=-1, keepdims=True) * scale + bself_ref[0, :, 0:1]
        m_old = m_ref[...]
        m_fin = jnp.maximum(m_old, s_new)
        a = jnp.exp(m_old - m_fin)
        p_new = jnp.exp(s_new - m_fin)
        l = a * l_ref[...] + p_new
        o_ref[0] = (a * acc_ref[...] + p_new * vnew_h) / l


def _s_attn(page_table, q, bias_past, bias_self, k_new, v_new, cache_k, cache_v, layer):
    nb, npages = page_table.shape
    nsplit = bias_past.shape[1]
    pps = npages // nsplit
    width = KVH_A * HD_A
    nrow = pps * PAGE * KVH_A
    blk3 = lambda *s: pl.BlockSpec((1,) + s, lambda b, h, pt: (b, 0, 0))
    return pl.pallas_call(
        functools.partial(_s_attn_kernel, layer=layer, pages_per_step=pps, nsplit=nsplit),
        grid_spec=pltpu.PrefetchScalarGridSpec(
            num_scalar_prefetch=1,
            grid=(nb, nsplit),
            in_specs=[blk3(H_A, HD_A),
                      pl.BlockSpec((1, 1, 1, nrow), lambda b, h, pt: (b, h, 0, 0)),
                      blk3(1, LANES), blk3(1, width), blk3(1, width),
                      pl.BlockSpec(memory_space=pl.ANY), pl.BlockSpec(memory_space=pl.ANY)],
            out_specs=blk3(H_A, HD_A),
            scratch_shapes=[pltpu.VMEM((2, nrow, HD_A), F32), pltpu.VMEM((2, nrow, HD_A), F32),
                            pltpu.SemaphoreType.DMA((2,)), pltpu.SemaphoreType.DMA((2,)),
                            pltpu.VMEM((H_A, 1), F32), pltpu.VMEM((H_A, 1), F32), pltpu.VMEM((H_A, HD_A), F32)],
        ),
        out_shape=jax.ShapeDtypeStruct((nb, H_A, HD_A), F32),
        compiler_params=_params(("arbitrary", "arbitrary")),
        name="sample_attn",
    )(page_table, q, bias_past, bias_self, k_new, v_new, cache_k, cache_v)


def kernel(x_prompt, x_sample, cache_a_k, cache_a_v, cache_a_kidx, state_b_k, state_b_v, state_conv, page_table, norm_mix, norm_ffn, w_in_a, q_norm_a, k_norm_a, w_out_a, norm_kv_b, w_kv_b, k_norm_b, w_q_b, q_norm_b, sinks_b, w_out_b, w_up, conv_w, conv_b, w_down):
    bp, seq, _ = x_prompt.shape
    bs, tdec, _ = x_sample.shape
    assert tdec == 1
    n_a = w_in_a.shape[0]
    depth = w_up.shape[0]
    npages = page_table.shape[1]
    past = npages * PAGE
    n_pool = cache_a_k.shape[1]
    tm_p = min(512, seq)
    tm_mlp = min(512, seq)
    tf = 256

    pos_p = jnp.arange(seq, dtype=I32)
    pos_s = jnp.full((bs,), past, I32)
    tabs = {}
    for name, pos in (("p", pos_p), ("s", pos_s)):
        tabs[name] = (_rope_tables(pos, HD_A), _rope_tables(pos, IDX_DIM),
                      _rope_tables(pos, IDX_DIM, pad_identity=LANES - IDX_DIM))

    row = lambda a: a.reshape(1, -1)
    tile2 = lambda a: jnp.tile(a.reshape(1, -1), (1, 2))
    w_in = jnp.pad(w_in_a, ((0, 0), (0, 0), (0, A_WIDTH_PAD - w_in_a.shape[2]))).astype(BF16)
    w_out_a_b = w_out_a.astype(BF16)
    w_up_b = w_up.astype(BF16)
    w_down_b = w_down.astype(BF16)
    w_kv_b_b = w_kv_b.astype(BF16)
    w_q_b_b = w_q_b.astype(BF16)
    w_out_b_b = w_out_b.astype(BF16)
    ck = cache_a_k.reshape(n_a, n_pool, PAGE * KVH_A, HD_A)
    cv = cache_a_v.reshape(n_a, n_pool, PAGE * KVH_A, HD_A)
    cidx_t = jnp.swapaxes(cache_a_kidx, 2, 3)
    buf_k = state_b_k.reshape(bs, WINDOW, KVH_B * HD_B)
    buf_v = state_b_v.reshape(bs, WINDOW, KVH_B * HD_B)

    hp = x_prompt.reshape(bp * seq, D_MODEL)
    hs = x_sample.reshape(bs, D_MODEL)
    a_k_p, a_v_p, a_ki_p, a_k_s, a_v_s, a_ki_s, conv_p, conv_s = [], [], [], [], [], [], [], []
    topk_s = min(TOPK_MAX, (past + 1) // 4)
    nsplit = 2 if npages % 2 == 0 else 1

    for l in range(depth):
        if l < n_a:
            t128, t64, tki = tabs["p"]
            q, k, v, kb, vt, qi, misc, ki2 = _proj_a(hp, row(norm_mix[l]), w_in[l], row(q_norm_a[l]),
                                                     row(k_norm_a[l]), t128, t64, tki, tm_p)
            o = _dsa_prompt(q, qi, misc, ki2, kb, vt, bp, seq, tm_p)
            hp = _out_res(o, w_out_a_b[l], hp, tm_p)
            a_k_p.append(k.reshape(bp, seq, KVH_A, HD_A))
            a_v_p.append(v.reshape(bp, seq, KVH_A, HD_A))
            a_ki_p.append(misc[:, :IDX_DIM].reshape(bp, seq, IDX_DIM))

            t128, t64, tki = tabs["s"]
            q, k, v, kb, vt, qi, misc, ki2 = _proj_a(hs, row(norm_mix[l]), w_in[l], row(q_norm_a[l]),
                                                     row(k_norm_a[l]), t128, t64, tki, bs)
            wi = misc[:, IDX_DIM:IDX_DIM + IDX_HEADS].reshape(bs, IDX_HEADS, 1)
            sc = _s_scores(page_table, qi.reshape(bs, IDX_HEADS, IDX_DIM), wi,
                           ki2[:, :IDX_DIM].reshape(bs, 1, IDX_DIM), cidx_t, l)
            bias = _s_select(sc.reshape(bs, past + LANES), past + 1, topk_s)
            bias_rows = jnp.repeat(bias[:, :past], KVH_A, axis=1)
            o = _s_attn(page_table, q.astype(F32).reshape(bs, H_A, HD_A),
                        bias_rows.reshape(bs, nsplit, 1, past * KVH_A // nsplit),
                        bias[:, past:].reshape(bs, 1, LANES),
                        k.reshape(bs, 1, KVH_A * HD_A), v.reshape(bs, 1, KVH_A * HD_A), ck, cv, l)
            hs = _out_res(o.reshape(bs, H_A * HD_A), w_out_a_b[l], hs, bs)
            a_k_s.append(k.reshape(bs, 1, KVH_A, HD_A))
            a_v_s.append(v.reshape(bs, 1, KVH_A, HD_A))
            a_ki_s.append(misc[:, :IDX_DIM].reshape(bs, 1, IDX_DIM))
        else:
            b = l - n_a
            sink = jnp.broadcast_to(sinks_b[b][:, None], (H_B, LANES))
            sink2 = jnp.repeat(sinks_b[b].reshape(H_B // 2, 2), WINDOW, axis=1)
            qp = _q_b(hp, row(norm_mix[l]), w_q_b_b[b], tile2(q_norm_b[b]), tabs["p"][1], tm_p, BF16)
            o = _swa_prompt(qp, kdup_p, vt_p, sink2, bp, seq)
            hp = _out_res(o, w_out_b_b[b], hp, tm_p)
            qs = _q_b(hs, row(norm_mix[l]), w_q_b_b[b], tile2(q_norm_b[b]), tabs["s"][1], bs, F32)
            res = _swa_sample(qs.reshape(bs, H_B, HD_B), buf_k, buf_v, kb_s.reshape(bs, 1, KVH_B * HD_B),
                              vb_s.reshape(bs, 1, KVH_B * HD_B), sink, emit_state=(b == 0))
            if b == 0:
                new_b_k_s, new_b_v_s = res[1], res[2]
            hs = _out_res(res[0].reshape(bs, H_B * HD_B), w_out_b_b[b], hs, bs)

        hp, tail_p = _mlp_prompt(hp, row(norm_ffn[l]), w_up_b[l], conv_w[l], row(conv_b[l]), w_down_b[l],
                                 seq, tm_mlp, tf)
        conv_p.append(tail_p)
        hs, gate_s = _mlp_sample(hs, state_conv[l, :, 0, :], state_conv[l, :, 1, :], row(norm_ffn[l]), w_up_b[l],
                                 conv_w[l], row(conv_b[l]), w_down_b[l], tf)
        conv_s.append(jnp.stack([state_conv[l, :, 1, :], gate_s], axis=1))

        if l == n_a - 1:
            kb_p, vb_p, kdup_p, vt_p = _kv_b(hp, row(norm_kv_b), w_kv_b_b, tile2(k_norm_b), tabs["p"][1], tm_p)
            kb_s, vb_s, _, _ = _kv_b(hs, row(norm_kv_b), w_kv_b_b, tile2(k_norm_b), tabs["s"][1], bs)

    kb_p4 = kb_p.reshape(bp, seq, KVH_B, HD_B)
    vb_p4 = vb_p.reshape(bp, seq, KVH_B, HD_B)
    return (hp.reshape(bp, seq, D_MODEL), hs.reshape(bs, 1, D_MODEL),
            jnp.stack(a_k_p), jnp.stack(a_v_p), jnp.stack(a_ki_p),
            jnp.stack(a_k_s), jnp.stack(a_v_s), jnp.stack(a_ki_s),
            kb_p4[:, -WINDOW:], vb_p4[:, -WINDOW:],
            new_b_k_s.reshape(bs, WINDOW, KVH_B, HD_B), new_b_v_s.reshape(bs, WINDOW, KVH_B, HD_B),
            jnp.stack(conv_p), jnp.stack(conv_s))
```

```python
import functools
import math

import jax
import jax.numpy as jnp
from jax import lax
from jax.experimental import pallas as pl
from jax.experimental.pallas import tpu as pltpu

F32 = jnp.float32
BF16 = jnp.bfloat16
I32 = jnp.int32

D_MODEL = 1024
PAGE = 128
H_A, HD_A, KVH_A = 8, 128, 2
G_A = H_A // KVH_A
IDX_HEADS, IDX_DIM = 16, 64
TOPK_MAX = 256
QB = 128
H_B, HD_B, KVH_B = 16, 64, 2
G_B = H_B // KVH_B
WINDOW = 128
D_FF = 2816
ROPE_THETA = 500000.0
EPS = 1e-6
LANES = 128
SUBLANES = 8
A_Q0, A_K0, A_V0, A_QI0, A_KI0 = 0, 1024, 1280, 1536, 2560
A_WIDTH_PAD = 2688
INT_MIN = -(2 ** 31)
NEG = -1e30
LOG2E = math.log2(math.e)
VMEM_LIMIT = 56 * 1024 * 1024

_NT = (((1,), (1,)), ((), ()))


def _dot(a, b):
    return jnp.dot(a, b, preferred_element_type=F32)


def _dot_nt(a, b):
    return lax.dot_general(a, b, _NT, preferred_element_type=F32)


def _params(sem):
    return pltpu.CompilerParams(dimension_semantics=sem, vmem_limit_bytes=VMEM_LIMIT)


def _resident(shape):
    return pl.BlockSpec(shape, lambda *_: (0,) * len(shape), pipeline_mode=pl.Buffered(1))


def _rms(x, g):
    return x * lax.rsqrt(jnp.mean(x * x, axis=-1, keepdims=True) + EPS) * g


def _head_norm(x, g, hd):
    x2 = x * x
    if hd == LANES:
        r = lax.rsqrt(jnp.sum(x2, axis=-1, keepdims=True) * (1.0 / hd) + EPS)
    else:
        lane = lax.broadcasted_iota(I32, x.shape, 1)
        lo = lane < hd
        s_lo = jnp.sum(jnp.where(lo, x2, 0.0), axis=-1, keepdims=True)
        s_hi = jnp.sum(jnp.where(lo, 0.0, x2), axis=-1, keepdims=True)
        r = jnp.where(lo, lax.rsqrt(s_lo * (1.0 / hd) + EPS), lax.rsqrt(s_hi * (1.0 / hd) + EPS))
    return x * r * g


def _rope(x, tab_ref, half):
    c, sa, sb = tab_ref[0], tab_ref[1], tab_ref[2]
    return x * c + pltpu.roll(x, LANES - half, 1) * sa + pltpu.roll(x, half, 1) * sb


def _rope_tables(pos, hd, pad_identity=0):
    rot = hd // 4
    half = rot // 2
    inv = ROPE_THETA ** (-jnp.arange(half, dtype=F32) / half)
    ang = pos.astype(F32)[:, None] * inv[None, :]
    cos, sin = jnp.cos(ang), jnp.sin(ang)
    t = pos.shape[0]
    ones = lambda n: jnp.ones((t, n), F32)
    zeros = lambda n: jnp.zeros((t, n), F32)
    c = jnp.concatenate([cos, cos, ones(hd - rot)], axis=1)
    sa = jnp.concatenate([-sin, zeros(hd - half)], axis=1)
    sb = jnp.concatenate([zeros(half), sin, zeros(hd - rot)], axis=1)
    if pad_identity:
        c = jnp.concatenate([c, ones(pad_identity)], axis=1)
        sa = jnp.concatenate([sa, zeros(pad_identity)], axis=1)
        sb = jnp.concatenate([sb, zeros(pad_identity)], axis=1)
    reps = LANES // c.shape[1]
    return jnp.stack([jnp.tile(a, (1, reps)) for a in (c, sa, sb)])


def _sort_key(x):
    bits = lax.bitcast_convert_type(x, I32)
    return jnp.where(bits >= 0, bits, bits ^ 0x7FFFFFFF)


def _radix_descend(count_ge, lo, nbits, k):
    def bit_step(i, lo):
        cand = lo ^ jnp.left_shift(jnp.int32(1), nbits - 1 - i)
        return jnp.where(count_ge(cand) >= k, cand, lo)

    return lax.fori_loop(0, nbits, bit_step, lo)


def _fold_lanes(m):
    part = m[:, 0:LANES]
    for t in range(1, m.shape[1] // LANES):
        part = part + m[:, t * LANES:(t + 1) * LANES]
    return part


def _fold_rows(x, op):
    vs = [x[i * SUBLANES:(i + 1) * SUBLANES] for i in range(x.shape[0] // SUBLANES)]
    while len(vs) > 1:
        vs = [op(vs[i], vs[i + 1]) if i + 1 < len(vs) else vs[i] for i in range(0, len(vs), 2)]
    return vs[0]


def _split_pair(blk):
    x = blk.astype(F32)
    lane = lax.broadcasted_iota(I32, x.shape, 1)
    lo = lane < LANES // 2
    both = jnp.concatenate([jnp.where(lo, x, 0.0), jnp.where(lo, 0.0, x)], axis=0)
    return jnp.transpose(both).astype(BF16)


def _proj_a_kernel(x_ref, g_ref, w_ref, qn_ref, kn_ref, t128_ref, t64_ref, tki_ref,
                   q_ref, k_ref, v_ref, kb_ref, vt_ref, qi_ref, misc_ref, ki2_ref):
    xn = _rms(x_ref[...], g_ref[...]).astype(BF16)
    y = _dot(xn, w_ref[...])
    qn, kn = qn_ref[...], kn_ref[...]
    for h in range(H_A):
        xh = _head_norm(y[:, A_Q0 + h * HD_A:A_Q0 + (h + 1) * HD_A], qn, HD_A)
        q_ref[:, h * HD_A:(h + 1) * HD_A] = _rope(xh, t128_ref, HD_A // 8).astype(BF16)
    for n in range(KVH_A):
        xh = _head_norm(y[:, A_K0 + n * HD_A:A_K0 + (n + 1) * HD_A], kn, HD_A)
        kh = _rope(xh, t128_ref, HD_A // 8)
        k_ref[:, n * HD_A:(n + 1) * HD_A] = kh
        kb_ref[:, n * HD_A:(n + 1) * HD_A] = kh.astype(BF16)
    v = y[:, A_V0:A_QI0]
    v_ref[...] = v
    vt_ref[0] = jnp.transpose(v).astype(BF16)
    for c in range(IDX_HEADS * IDX_DIM // LANES):
        xh = y[:, A_QI0 + c * LANES:A_QI0 + (c + 1) * LANES]
        qi_ref[:, c * LANES:(c + 1) * LANES] = _rope(xh, t64_ref, IDX_DIM // 8).astype(BF16)
    m = _rope(y[:, A_KI0:A_WIDTH_PAD], tki_ref, IDX_DIM // 8)
    misc_ref[...] = m
    lane = lax.broadcasted_iota(I32, m.shape, 1)
    ki2_ref[...] = jnp.where(lane < IDX_DIM, m, pltpu.roll(m, IDX_DIM, 1)).astype(BF16)


def _proj_a(h, g, w, qn, kn, t128, t64, tki, tm):
    m = h.shape[0]
    nt = t128.shape[1] // tm
    row = lambda i: (i, 0)
    const = lambda i: (0, 0)
    tab = lambda i: (0, i % nt, 0)
    kvw = KVH_A * HD_A
    out_shapes = (
        jax.ShapeDtypeStruct((m, H_A * HD_A), BF16),
        jax.ShapeDtypeStruct((m, kvw), F32),
        jax.ShapeDtypeStruct((m, kvw), F32),
        jax.ShapeDtypeStruct((m, kvw), BF16),
        jax.ShapeDtypeStruct((m // tm, kvw, tm), BF16),
        jax.ShapeDtypeStruct((m, IDX_HEADS * IDX_DIM), BF16),
        jax.ShapeDtypeStruct((m, LANES), F32),
        jax.ShapeDtypeStruct((m, LANES), BF16),
    )
    out_specs = [pl.BlockSpec((tm, s.shape[1]), row) if len(s.shape) == 2
                 else pl.BlockSpec((1, kvw, tm), lambda i: (i, 0, 0)) for s in out_shapes]
    return pl.pallas_call(
        _proj_a_kernel,
        grid=(m // tm,),
        in_specs=[
            pl.BlockSpec((tm, D_MODEL), row),
            pl.BlockSpec((1, D_MODEL), const),
            _resident((D_MODEL, A_WIDTH_PAD)),
            pl.BlockSpec((1, LANES), const),
            pl.BlockSpec((1, LANES), const),
            pl.BlockSpec((3, tm, LANES), tab),
            pl.BlockSpec((3, tm, LANES), tab),
            pl.BlockSpec((3, tm, LANES), tab),
        ],
        out_specs=out_specs,
        out_shape=out_shapes,
        compiler_params=_params(("parallel",)),
        name="proj_a",
    )(h, g, w, qn, kn, t128, t64, tki)


def _dsa_prompt_kernel(q_ref, qi_ref, w_ref, ki2_ref, kb_ref, vt_ref, o_ref, key_ref, bias_ref, t_ref, *, ch, topk):
    j = pl.program_id(1)
    t0 = j * QB
    nch = (t0 + QB + ch - 1) // ch
    wt = jnp.transpose(w_ref[...])[IDX_DIM:IDX_DIM + IDX_HEADS, :] * (IDX_DIM ** -0.5 * IDX_HEADS ** -0.5)
    wrows = [wt[h:h + 1, :] for h in range(IDX_HEADS)]
    npair = IDX_HEADS // 2
    pairs = [_split_pair(qi_ref[:, c * LANES:(c + 1) * LANES]) for c in range(npair)]
    krow = lax.broadcasted_iota(I32, (ch, QB), 0)
    qcol = lax.broadcasted_iota(I32, (ch, QB), 1) + t0

    def score_chunk(c, carry):
        ks = pl.multiple_of(c * ch, ch)
        ki2 = ki2_ref[pl.ds(ks, ch), :]
        acc = jnp.zeros((ch, QB), F32)
        for cc in range(npair):
            d = _dot(ki2, pairs[cc])
            acc = acc + jnp.maximum(d[:, :QB], 0.0) * wrows[2 * cc] + jnp.maximum(d[:, QB:], 0.0) * wrows[2 * cc + 1]
        key_ref[c] = jnp.where(krow + ks <= qcol, _sort_key(acc), INT_MIN)
        return carry

    lax.fori_loop(0, nch, score_chunk, 0)

    def count_ge(cand):
        def body(c, cnt):
            return cnt + _fold_rows(jnp.where(key_ref[c] >= cand, 1.0, 0.0), jnp.add)

        cnt = lax.fori_loop(0, nch, body, jnp.zeros((SUBLANES, QB), F32))
        return jnp.sum(cnt, axis=0, keepdims=True)

    if ch % TOPK_MAX == 0:
        def group_max(c, gmax):
            kk = key_ref[c]
            for s in range(ch // TOPK_MAX):
                gmax = jnp.maximum(gmax, kk[s * TOPK_MAX:(s + 1) * TOPK_MAX])
            return gmax

        gmax = lax.fori_loop(0, nch, group_max, jnp.full((TOPK_MAX, QB), INT_MIN, I32))
        upper = jnp.max(gmax, axis=0, keepdims=True)
        lower = jnp.min(gmax, axis=0, keepdims=True)
        nbits = jnp.max(32 - lax.clz(lower ^ upper))
        keep = jnp.where(nbits >= 32, 0, jnp.left_shift(jnp.int32(-1), jnp.minimum(nbits, 31)))
        lo0 = ((lower ^ INT_MIN) & keep) ^ INT_MIN
    else:
        nbits = 32
        lo0 = jnp.full((1, QB), INT_MIN, I32)

    thr = jnp.maximum(_radix_descend(count_ge, lo0, nbits, topk), INT_MIN + 1)

    def bias_chunk(c, carry):
        bias_ref[c] = jnp.where(key_ref[c] >= thr, 0.0, NEG)
        return carry

    lax.fori_loop(0, nch, bias_chunk, 0)

    cfac = HD_A ** -0.5 * LOG2E
    qns = [jnp.concatenate([jnp.transpose(q_ref[:, (n * G_A + g) * HD_A:(n * G_A + g + 1) * HD_A].astype(F32))
                            for g in range(G_A)], axis=1).astype(BF16) for n in range(KVH_A)]

    def logits_chunk(c, mxs):
        ks = pl.multiple_of(c * ch, ch)
        b4 = jnp.concatenate([bias_ref[c]] * G_A, axis=1)
        out = []
        for n in range(KVH_A):
            kc = kb_ref[pl.ds(ks, ch), n * HD_A:(n + 1) * HD_A]
            t = _dot(kc, qns[n]) * cfac + b4
            t_ref[n, c] = t
            out.append(jnp.maximum(mxs[n], _fold_rows(t, jnp.maximum)))
        return tuple(out)

    mxs = lax.fori_loop(0, nch, logits_chunk, (jnp.full((SUBLANES, G_A * QB), NEG, F32),) * KVH_A)
    ms = [jnp.max(mx, axis=0, keepdims=True) for mx in mxs]

    def pv_chunk(c, carry):
        out = []
        for n in range(KVH_A):
            lsum, acc = carry[n]
            p = jnp.exp2(t_ref[n, c] - ms[n])
            acc = acc + _dot(vt_ref[c, n * HD_A:(n + 1) * HD_A, :], p.astype(BF16))
            out.append((lsum + _fold_rows(p, jnp.add), acc))
        return tuple(out)

    init = ((jnp.zeros((SUBLANES, G_A * QB), F32), jnp.zeros((HD_A, G_A * QB), F32)),) * KVH_A
    res = lax.fori_loop(0, nch, pv_chunk, init)
    for n in range(KVH_A):
        lsum, acc = res[n]
        out = acc / jnp.sum(lsum, axis=0, keepdims=True)
        for g in range(G_A):
            h = n * G_A + g
            o_ref[:, h * HD_A:(h + 1) * HD_A] = jnp.transpose(out[:, g * QB:(g + 1) * QB]).astype(BF16)


def _dsa_prompt(q, qi, misc, ki2, kb, vt, nbatch, seq, ch):
    nq = seq // QB
    nchunks = seq // ch
    topk = min(TOPK_MAX, seq // 4)
    kvw = KVH_A * HD_A
    qblk = lambda b, j: (b * nq + j, 0)
    bat = lambda b, j: (b, 0)
    return pl.pallas_call(
        functools.partial(_dsa_prompt_kernel, ch=ch, topk=topk),
        grid=(nbatch, nq),
        in_specs=[
            pl.BlockSpec((QB, H_A * HD_A), qblk),
            pl.BlockSpec((QB, IDX_HEADS * IDX_DIM), qblk),
            pl.BlockSpec((QB, LANES), qblk),
            pl.BlockSpec((seq, LANES), bat),
            pl.BlockSpec((seq, kvw), bat),
            pl.BlockSpec((nchunks, kvw, ch), lambda b, j: (b, 0, 0)),
        ],
        out_specs=pl.BlockSpec((QB, H_A * HD_A), qblk),
        out_shape=jax.ShapeDtypeStruct((nbatch * seq, H_A * HD_A), BF16),
        scratch_shapes=[pltpu.VMEM((nchunks, ch, QB), I32), pltpu.VMEM((nchunks, ch, QB), F32),
                        pltpu.VMEM((KVH_A, nchunks, ch, G_A * QB), F32)],
        compiler_params=_params(("parallel", "parallel")),
        name="dsa_prompt",
    )(q, qi, misc, ki2, kb, vt)


def _out_res_kernel(o_ref, w_ref, h_ref, y_ref):
    y_ref[...] = h_ref[...] + _dot(o_ref[...].astype(BF16), w_ref[...])


def _out_res(o, w, h, tm):
    m, kdim = o.shape
    row = lambda i: (i, 0)
    return pl.pallas_call(
        _out_res_kernel,
        grid=(m // tm,),
        in_specs=[pl.BlockSpec((tm, kdim), row), _resident((kdim, D_MODEL)),
                  pl.BlockSpec((tm, D_MODEL), row)],
        out_specs=pl.BlockSpec((tm, D_MODEL), row),
        out_shape=jax.ShapeDtypeStruct((m, D_MODEL), F32),
        compiler_params=_params(("parallel",)),
        name="out_res",
    )(o, w, h)


HALO = 16


def _silu(x):
    return x * jax.nn.sigmoid(x)


def _mlp_prompt_kernel(h_ref, halo_ref, o_ref, ohalo_ref, wo_ref, g_ref, wup_ref, cw_ref, cb_ref, wd_ref,
                       out_ref, tail_ref, xn_ref, gs_ref, act_ref, *, tm, tf, tiles_per_seq):
    i = pl.program_id(0)
    g = g_ref[...]
    wo = wo_ref[...]
    xn_ref[0:HALO, :] = _rms(halo_ref[...] + _dot(ohalo_ref[...], wo), g).astype(BF16)
    h1 = h_ref[...] + _dot(o_ref[...], wo)
    out_ref[...] = h1
    xn_ref[HALO:HALO + tm, :] = _rms(h1, g).astype(BF16)
    first = i % tiles_per_seq == 0
    for f in range(D_FF // tf):
        cols = slice(f * tf, (f + 1) * tf)
        gs_ref[...] = _dot(xn_ref[...], wup_ref[:, cols])
        up = _dot(xn_ref[HALO:HALO + tm, :], wup_ref[:, D_FF + f * tf:D_FF + (f + 1) * tf])

        @pl.when(first)
        def _():
            gs_ref[0:HALO, :] = jnp.zeros((HALO, tf), F32)

        conv = (cb_ref[:, cols] + cw_ref[0:1, cols] * gs_ref[HALO - 2:HALO - 2 + tm, :]
                + cw_ref[1:2, cols] * gs_ref[HALO - 1:HALO - 1 + tm, :]
                + cw_ref[2:3, cols] * gs_ref[HALO:HALO + tm, :])
        act_ref[:, cols] = (_silu(conv) * up).astype(BF16)
        tail_ref[0, :, cols] = gs_ref[tm:tm + HALO, :]
    out_ref[...] += _dot(act_ref[...], wd_ref[...])


def _mlp_prompt(h, o, w_o, g, w_up, cw, cb, w_down, seq, tm, tf):
    m = h.shape[0]
    kdim = o.shape[1]
    tiles_per_seq = seq // tm
    hb = tm // HALO
    row = lambda i: (i, 0)
    halo = lambda i: (jnp.maximum(i * hb - 1, 0), 0)
    out, tails = pl.pallas_call(
        functools.partial(_mlp_prompt_kernel, tm=tm, tf=tf, tiles_per_seq=tiles_per_seq),
        grid=(m // tm,),
        in_specs=[
            pl.BlockSpec((tm, D_MODEL), row),
            pl.BlockSpec((HALO, D_MODEL), halo),
            pl.BlockSpec((tm, kdim), row),
            pl.BlockSpec((HALO, kdim), halo),
            _resident((kdim, D_MODEL)),
            _resident((1, D_MODEL)),
            _resident((D_MODEL, 2 * D_FF)),
            _resident((3, D_FF)),
            _resident((1, D_FF)),
            _resident((D_FF, D_MODEL)),
        ],
        out_specs=[pl.BlockSpec((tm, D_MODEL), lambda i: (i, 0)),
                   pl.BlockSpec((1, HALO, D_FF), lambda i: (i, 0, 0))],
        out_shape=(jax.ShapeDtypeStruct((m, D_MODEL), F32),
                   jax.ShapeDtypeStruct((m // tm, HALO, D_FF), F32)),
        scratch_shapes=[pltpu.VMEM((tm + HALO, D_MODEL), BF16), pltpu.VMEM((tm + HALO, tf), F32),
                        pltpu.VMEM((tm, D_FF), BF16)],
        compiler_params=_params(("parallel",)),
        name="mlp_prompt",
    )(h, h, o, o, w_o, g, w_up, cw, cb, w_down)
    nb = m // seq
    tail = tails.reshape(nb, tiles_per_seq, HALO, D_FF)[:, -1, HALO - 2:, :]
    return out, tail


def _mlp_sample_kernel(h_ref, s0_ref, s1_ref, g_ref, wg_ref, wu_ref, cw_ref, cb_ref, wd_ref,
                       out_ref, gate_ref, xn_ref, acc_ref):
    j = pl.program_id(0)

    @pl.when(j == 0)
    def _():
        xn_ref[...] = _rms(h_ref[...], g_ref[...]).astype(BF16)
        acc_ref[...] = jnp.zeros_like(acc_ref)

    xn = xn_ref[...]
    gate = _dot(xn, wg_ref[...])
    up = _dot(xn, wu_ref[...])
    gate_ref[...] = gate
    conv = cb_ref[...] + cw_ref[0:1, :] * s0_ref[...] + cw_ref[1:2, :] * s1_ref[...] + cw_ref[2:3, :] * gate
    act = (_silu(conv) * up).astype(BF16)
    acc_ref[...] += _dot(act, wd_ref[...])

    @pl.when(j == pl.num_programs(0) - 1)
    def _():
        out_ref[...] = h_ref[...] + acc_ref[...]


def _mlp_sample(h, s0, s1, g, w_up, cw, cb, w_down, tf):
    m = h.shape[0]
    nf = D_FF // tf
    full = lambda j: (0, 0)
    col = lambda j: (0, j)
    return pl.pallas_call(
        _mlp_sample_kernel,
        grid=(nf,),
        in_specs=[
            pl.BlockSpec((m, D_MODEL), full),
            pl.BlockSpec((m, tf), col),
            pl.BlockSpec((m, tf), col),
            pl.BlockSpec((1, D_MODEL), full),
            pl.BlockSpec((D_MODEL, tf), col),
            pl.BlockSpec((D_MODEL, tf), lambda j: (0, nf + j)),
            pl.BlockSpec((3, tf), col),
            pl.BlockSpec((1, tf), col),
            pl.BlockSpec((tf, D_MODEL), lambda j: (j, 0)),
        ],
        out_specs=[pl.BlockSpec((m, D_MODEL), full), pl.BlockSpec((m, tf), col)],
        out_shape=(jax.ShapeDtypeStruct((m, D_MODEL), F32), jax.ShapeDtypeStruct((m, D_FF), F32)),
        scratch_shapes=[pltpu.VMEM((m, D_MODEL), BF16), pltpu.VMEM((m, D_MODEL), F32)],
        compiler_params=_params(("arbitrary",)),
        name="mlp_sample",
    )(h, s0, s1, g, w_up, w_up, cw, cb, w_down)


def _kv_b_kernel(x_ref, g_ref, w_ref, kn_ref, t64_ref, k_ref, v_ref, kdup_ref, vt_ref):
    xn = _rms(x_ref[...], g_ref[...]).astype(BF16)
    y = _dot(xn, w_ref[...])
    k = _rope(_head_norm(y[:, :LANES], kn_ref[...], HD_B), t64_ref, HD_B // 8)
    v = y[:, LANES:]
    k_ref[...] = k
    v_ref[...] = v
    lane = lax.broadcasted_iota(I32, k.shape, 1)
    kr = pltpu.roll(k, HD_B, 1)
    kdup_ref[:, :LANES] = jnp.where(lane < HD_B, k, kr).astype(BF16)
    kdup_ref[:, LANES:] = jnp.where(lane < HD_B, kr, k).astype(BF16)
    for s in range(vt_ref.shape[0]):
        vt_ref[s] = jnp.transpose(v[s * WINDOW:(s + 1) * WINDOW, :]).astype(BF16)


def _kv_b(h, g, w, kn, t64, tm):
    m = h.shape[0]
    nt = t64.shape[1] // tm
    row = lambda i: (i, 0)
    const = lambda i: (0, 0)
    width = KVH_B * HD_B
    nsub = max(tm // WINDOW, 1)
    vt_rows = min(tm, WINDOW)
    shapes = (jax.ShapeDtypeStruct((m, width), F32), jax.ShapeDtypeStruct((m, width), F32),
              jax.ShapeDtypeStruct((m, 2 * width), BF16),
              jax.ShapeDtypeStruct((m // vt_rows, width, vt_rows), BF16))
    return pl.pallas_call(
        _kv_b_kernel,
        grid=(m // tm,),
        in_specs=[pl.BlockSpec((tm, D_MODEL), row), pl.BlockSpec((1, D_MODEL), const),
                  _resident((D_MODEL, 2 * width)), pl.BlockSpec((1, LANES), const),
                  pl.BlockSpec((3, tm, LANES), lambda i: (0, i % nt, 0))],
        out_specs=[pl.BlockSpec((tm, width), row), pl.BlockSpec((tm, width), row),
                   pl.BlockSpec((tm, 2 * width), row),
                   pl.BlockSpec((nsub, width, vt_rows), lambda i: (i, 0, 0))],
        out_shape=shapes,
        compiler_params=_params(("parallel",)),
        name="kv_b",
    )(h, g, w, kn, t64)


def _q_b_kernel(x_ref, g_ref, w_ref, qn_ref, t64_ref, q_ref):
    xn = _rms(x_ref[...], g_ref[...]).astype(BF16)
    y = _dot(xn, w_ref[...])
    qn = qn_ref[...]
    for c in range(H_B * HD_B // LANES):
        xh = _head_norm(y[:, c * LANES:(c + 1) * LANES], qn, HD_B)
        q_ref[:, c * LANES:(c + 1) * LANES] = (_rope(xh, t64_ref, HD_B // 8) * (HD_B ** -0.5)).astype(q_ref.dtype)


def _q_b(h, g, w, qn, t64, tm, dtype):
    m = h.shape[0]
    nt = t64.shape[1] // tm
    row = lambda i: (i, 0)
    const = lambda i: (0, 0)
    return pl.pallas_call(
        _q_b_kernel,
        grid=(m // tm,),
        in_specs=[pl.BlockSpec((tm, D_MODEL), row), pl.BlockSpec((1, D_MODEL), const),
                  _resident((D_MODEL, H_B * HD_B)), pl.BlockSpec((1, LANES), const),
                  pl.BlockSpec((3, tm, LANES), lambda i: (0, i % nt, 0))],
        out_specs=pl.BlockSpec((tm, H_B * HD_B), row),
        out_shape=jax.ShapeDtypeStruct((m, H_B * HD_B), dtype),
        compiler_params=_params(("parallel",)),
        name="q_b",
    )(h, g, w, qn, t64)


def _swa_prompt_kernel(q_ref, kc_ref, kp_ref, vtc_ref, vtp_ref, sink_ref, o_ref):
    j = pl.program_id(1)
    kr = lax.broadcasted_iota(I32, (2 * WINDOW, WINDOW), 0)
    qc = lax.broadcasted_iota(I32, (2 * WINDOW, WINDOW), 1)
    rel = kr - WINDOW - qc
    ok = (rel <= 0) & (rel >= -WINDOW) & ((kr >= WINDOW) | (j > 0))
    bias = jnp.where(ok, 0.0, NEG)
    bias2 = jnp.concatenate([bias, bias], axis=1)
    kband = jnp.concatenate([kp_ref[...], kc_ref[...]], axis=0)
    vt = jnp.concatenate([vtp_ref[0], vtc_ref[0]], axis=1)
    npair = H_B // 2
    kv_of = [(2 * c) // G_B for c in range(npair)]
    logits = [_dot(kband[:, kv_of[c] * LANES:(kv_of[c] + 1) * LANES],
                   _split_pair(q_ref[:, c * LANES:(c + 1) * LANES])) + bias2 for c in range(npair)]
    probs, rdenom = [], []
    for c in range(npair):
        sink = sink_ref[c:c + 1, :]
        m = jnp.maximum(jnp.max(logits[c], axis=0, keepdims=True), sink)
        p = jnp.exp(logits[c] - m)
        rdenom.append(1.0 / (jnp.sum(p, axis=0, keepdims=True) + jnp.exp(sink - m)))
        probs.append(p.astype(BF16))
    outs = [_dot(vt[kv_of[c] * HD_B:(kv_of[c] + 1) * HD_B, :], probs[c]) * rdenom[c]
            for c in range(npair)]
    for c in range(npair):
        stacked = jnp.concatenate([outs[c][:, :WINDOW], outs[c][:, WINDOW:]], axis=0)
        o_ref[:, c * LANES:(c + 1) * LANES] = jnp.transpose(stacked).astype(BF16)


def _swa_prompt(q, kdup, vt, sink2, nbatch, seq):
    nblk = seq // WINDOW
    cur = lambda b, j: (b * nblk + j, 0)
    prev = lambda b, j: (b * nblk + jnp.maximum(j - 1, 0), 0)
    cur3 = lambda b, j: (b * nblk + j, 0, 0)
    prev3 = lambda b, j: (b * nblk + jnp.maximum(j - 1, 0), 0, 0)
    width = KVH_B * HD_B
    return pl.pallas_call(
        _swa_prompt_kernel,
        grid=(nbatch, nblk),
        in_specs=[pl.BlockSpec((WINDOW, H_B * HD_B), cur),
                  pl.BlockSpec((WINDOW, 2 * width), cur), pl.BlockSpec((WINDOW, 2 * width), prev),
                  pl.BlockSpec((1, width, WINDOW), cur3), pl.BlockSpec((1, width, WINDOW), prev3),
                  pl.BlockSpec((H_B // 2, 2 * WINDOW), lambda b, j: (0, 0))],
        out_specs=pl.BlockSpec((WINDOW, H_B * HD_B), cur),
        out_shape=jax.ShapeDtypeStruct((nbatch * seq, H_B * HD_B), BF16),
        compiler_params=_params(("parallel", "parallel")),
        name="swa_prompt",
    )(q, kdup, kdup, vt, vt, sink2)


SWA_SAMPLE_ROWS = 8


def _swa_sample_kernel(q_ref, bk_ref, bv_ref, kn_ref, vn_ref, sink_ref, o_ref, *state_refs):
    rowh = lax.broadcasted_iota(I32, (H_B, 1), 0)
    first = rowh < G_B
    sink = sink_ref[:, 0:1]
    for i in range(q_ref.shape[0]):
        q = q_ref[i]
        bk, bv = bk_ref[i], bv_ref[i]
        knew, vnew = kn_ref[i], vn_ref[i]
        qb = q.astype(BF16)
        s_past = jnp.where(first, _dot_nt(qb, bk[:, :HD_B].astype(BF16)), _dot_nt(qb, bk[:, HD_B:].astype(BF16)))
        knew_h = jnp.where(first, knew[:, :HD_B], knew[:, HD_B:])
        vnew_h = jnp.where(first, vnew[:, :HD_B], vnew[:, HD_B:])
        s_new = jnp.sum(q * knew_h, axis=-1, keepdims=True)
        m = jnp.maximum(jnp.maximum(jnp.max(s_past, axis=-1, keepdims=True), s_new), sink)
        p_past = jnp.exp(s_past - m)
        p_new = jnp.exp(s_new - m)
        denom = jnp.sum(p_past, axis=-1, keepdims=True) + p_new + jnp.exp(sink - m)
        pb = (p_past / denom).astype(BF16)
        o_past = jnp.where(first, _dot(pb, bv[:, :HD_B].astype(BF16)), _dot(pb, bv[:, HD_B:].astype(BF16)))
        o_ref[i] = o_past + (p_new / denom) * vnew_h
        if state_refs:
            nk_ref, nv_ref = state_refs
            nk_ref[i] = jnp.concatenate([bk[1:], knew], axis=0)
            nv_ref[i] = jnp.concatenate([bv[1:], vnew], axis=0)


def _swa_sample(q, buf_k, buf_v, k_new, v_new, sinks, emit_state):
    nb = q.shape[0]
    width = KVH_B * HD_B
    rows = SWA_SAMPLE_ROWS if nb % SWA_SAMPLE_ROWS == 0 else 1
    blk3 = lambda *s: pl.BlockSpec((rows,) + s, lambda b: (b, 0, 0))
    out_shape = [jax.ShapeDtypeStruct((nb, H_B, HD_B), F32)]
    out_specs = [blk3(H_B, HD_B)]
    if emit_state:
        out_shape += [jax.ShapeDtypeStruct((nb, WINDOW, width), F32)] * 2
        out_specs += [blk3(WINDOW, width)] * 2
    return pl.pallas_call(
        _swa_sample_kernel,
        grid=(nb // rows,),
        in_specs=[blk3(H_B, HD_B), blk3(WINDOW, width), blk3(WINDOW, width), blk3(1, width), blk3(1, width),
                  pl.BlockSpec((H_B, LANES), lambda b: (0, 0))],
        out_specs=out_specs,
        out_shape=out_shape,
        compiler_params=_params(("parallel",)),
        name="swa_sample",
    )(q, buf_k, buf_v, k_new, v_new, sinks)


def _s_scores_kernel(pt_ref, qi_ref, w_ref, kin_ref, cache_ref, out_ref, buf_ref, sem, *, layer, npages):
    b = pl.program_id(0)
    nb = pl.num_programs(0)
    slot = b % 2

    def copies(bb, sl):
        return [pltpu.make_async_copy(cache_ref.at[layer, pt_ref[bb, p]],
                                      buf_ref.at[sl, :, pl.ds(p * PAGE, PAGE)], sem.at[sl])
                for p in range(npages)]

    @pl.when(b == 0)
    def _():
        for cp in copies(0, 0):
            cp.start()

    @pl.when(b + 1 < nb)
    def _():
        for cp in copies(b + 1, 1 - slot):
            cp.start()

    for cp in copies(b, slot):
        cp.wait()

    qi = qi_ref[0]
    w = w_ref[0] * (IDX_DIM ** -0.5 * IDX_HEADS ** -0.5)
    past = npages * PAGE
    d_past = _dot(qi, buf_ref[slot].astype(BF16))
    d_new = _dot_nt(qi, jnp.broadcast_to(kin_ref[0], (LANES, IDX_DIM)))
    out_ref[0, :, 0:past] = jnp.sum(jnp.maximum(d_past, 0.0) * w, axis=0, keepdims=True)
    out_ref[0, :, past:past + LANES] = jnp.sum(jnp.maximum(d_new, 0.0) * w, axis=0, keepdims=True)


def _s_scores(page_table, qi, wi, ki_new, cache_t, layer):
    nb, npages = page_table.shape
    past = npages * PAGE
    blk3 = lambda *s: pl.BlockSpec((1,) + s, lambda b, pt: (b, 0, 0))
    return pl.pallas_call(
        functools.partial(_s_scores_kernel, layer=layer, npages=npages),
        grid_spec=pltpu.PrefetchScalarGridSpec(
            num_scalar_prefetch=1,
            grid=(nb,),
            in_specs=[blk3(IDX_HEADS, IDX_DIM), blk3(IDX_HEADS, 1), blk3(1, IDX_DIM),
                      pl.BlockSpec(memory_space=pl.ANY)],
            out_specs=blk3(1, past + LANES),
            scratch_shapes=[pltpu.VMEM((2, IDX_DIM, past), F32), pltpu.SemaphoreType.DMA((2,))],
        ),
        out_shape=jax.ShapeDtypeStruct((nb, 1, past + LANES), F32),
        compiler_params=_params(("arbitrary",)),
        name="sample_scores",
    )(page_table, qi, wi, ki_new, cache_t)


def _s_select_kernel(sc_ref, bias_ref, key_ref, *, nvalid, topk, ch):
    rows, width = sc_ref.shape
    nch = width // ch
    for c in range(nch):
        lane = lax.broadcasted_iota(I32, (rows, ch), 1) + c * ch
        key_ref[:, c * ch:(c + 1) * ch] = jnp.where(lane < nvalid, _sort_key(sc_ref[:, c * ch:(c + 1) * ch]), INT_MIN)

    def count_ge(cand):
        cnt = jnp.zeros((rows, LANES), F32)
        for c in range(nch):
            cnt = cnt + _fold_lanes(jnp.where(key_ref[:, c * ch:(c + 1) * ch] >= cand, 1.0, 0.0))
        return jnp.sum(cnt, axis=1, keepdims=True)

    thr = _radix_descend(count_ge, jnp.full((rows, 1), INT_MIN, I32), 32, topk)
    thr = jnp.maximum(thr, INT_MIN + 1)
    for c in range(nch):
        bias_ref[:, c * ch:(c + 1) * ch] = jnp.where(key_ref[:, c * ch:(c + 1) * ch] >= thr, 0.0, NEG)


def _s_select(sc, nvalid, topk):
    rows, width = sc.shape
    ch = LANES * 5 if width % (LANES * 5) == 0 else LANES
    return pl.pallas_call(
        functools.partial(_s_select_kernel, nvalid=nvalid, topk=topk, ch=ch),
        out_shape=jax.ShapeDtypeStruct((rows, width), F32),
        scratch_shapes=[pltpu.VMEM((rows, width), I32)],
        compiler_params=pltpu.CompilerParams(vmem_limit_bytes=VMEM_LIMIT),
        name="sample_select",
    )(sc)


def _s_attn_kernel(pt_ref, q_ref, bias_ref, bself_ref, kn_ref, vn_ref, kc_ref, vc_ref, o_ref,
                   kbuf, vbuf, ksem, vsem, m_ref, l_ref, acc_ref, *, layer, pages_per_step, nsplit):
    b = pl.program_id(0)
    hf = pl.program_id(1)
    step = b * nsplit + hf
    nsteps = pl.num_programs(0) * nsplit
    slot = step % 2
    prow = PAGE * KVH_A

    def copies(st, sl):
        bb = st // nsplit
        p0 = (st % nsplit) * pages_per_step
        out = []
        for cache, buf, sem in ((kc_ref, kbuf, ksem), (vc_ref, vbuf, vsem)):
            out += [pltpu.make_async_copy(cache.at[layer, pt_ref[bb, p0 + p]],
                                          buf.at[sl, pl.ds(p * prow, prow)], sem.at[sl])
                    for p in range(pages_per_step)]
        return out

    @pl.when(step == 0)
    def _():
        for cp in copies(0, 0):
            cp.start()

    @pl.when(step + 1 < nsteps)
    def _():
        for cp in copies(step + 1, 1 - slot):
            cp.start()

    for cp in copies(step, slot):
        cp.wait()

    @pl.when(hf == 0)
    def _():
        m_ref[...] = jnp.full(m_ref.shape, NEG, F32)
        l_ref[...] = jnp.zeros_like(l_ref)
        acc_ref[...] = jnp.zeros_like(acc_ref)

    scale = HD_A ** -0.5
    q = q_ref[0]
    nrow = pages_per_step * prow
    head_kv = jnp.right_shift(lax.broadcasted_iota(I32, (H_A, nrow), 0), G_A.bit_length() - 1)
    col_kv = jnp.bitwise_and(lax.broadcasted_iota(I32, (H_A, nrow), 1), KVH_A - 1)
    s = _dot_nt(q.astype(BF16), kbuf[slot].astype(BF16)) * scale + bias_ref[0, 0]
    s = jnp.where(head_kv == col_kv, s, NEG)
    m = m_ref[...]
    m_new = jnp.maximum(m, jnp.max(s, axis=-1, keepdims=True))
    p = jnp.exp(s - m_new)
    alpha = jnp.exp(m - m_new)
    l_ref[...] = alpha * l_ref[...] + jnp.sum(p, axis=-1, keepdims=True)
    acc_ref[...] = alpha * acc_ref[...] + _dot(p.astype(BF16), vbuf[slot].astype(BF16))
    m_ref[...] = m_new

    @pl.when(hf == nsplit - 1)
    def _():
        first = lax.broadcasted_iota(I32, (H_A, 1), 0) < G_A
        knew, vnew = kn_ref[0], vn_ref[0]
        knew_h = jnp.where(first, knew[:, :HD_A], knew[:, HD_A:])
        vnew_h = jnp.where(first, vnew[:, :HD_A], vnew[:, HD_A:])
        s_new = jnp.sum(q * knew_h, axis=-1, keepdims=True) * scale + bself_ref[0, :, 0:1]
        m_old = m_ref[...]
        m_fin = jnp.maximum(m_old, s_new)
        a = jnp.exp(m_old - m_fin)
        p_new = jnp.exp(s_new - m_fin)
        l = a * l_ref[...] + p_new
        o_ref[0] = (a * acc_ref[...] + p_new * vnew_h) / l


def _s_attn(page_table, q, bias_past, bias_self, k_new, v_new, cache_k, cache_v, layer):
    nb, npages = page_table.shape
    nsplit = bias_past.shape[1]
    pps = npages // nsplit
    width = KVH_A * HD_A
    nrow = pps * PAGE * KVH_A
    blk3 = lambda *s: pl.BlockSpec((1,) + s, lambda b, h, pt: (b, 0, 0))
    return pl.pallas_call(
        functools.partial(_s_attn_kernel, layer=layer, pages_per_step=pps, nsplit=nsplit),
        grid_spec=pltpu.PrefetchScalarGridSpec(
            num_scalar_prefetch=1,
            grid=(nb, nsplit),
            in_specs=[blk3(H_A, HD_A),
                      pl.BlockSpec((1, 1, 1, nrow), lambda b, h, pt: (b, h, 0, 0)),
                      blk3(1, LANES), blk3(1, width), blk3(1, width),
                      pl.BlockSpec(memory_space=pl.ANY), pl.BlockSpec(memory_space=pl.ANY)],
            out_specs=blk3(H_A, HD_A),
            scratch_shapes=[pltpu.VMEM((2, nrow, HD_A), F32), pltpu.VMEM((2, nrow, HD_A), F32),
                            pltpu.SemaphoreType.DMA((2,)), pltpu.SemaphoreType.DMA((2,)),
                            pltpu.VMEM((H_A, 1), F32), pltpu.VMEM((H_A, 1), F32), pltpu.VMEM((H_A, HD_A), F32)],
        ),
        out_shape=jax.ShapeDtypeStruct((nb, H_A, HD_A), F32),
        compiler_params=_params(("arbitrary", "arbitrary")),
        name="sample_attn",
    )(page_table, q, bias_past, bias_self, k_new, v_new, cache_k, cache_v)


def kernel(x_prompt, x_sample, cache_a_k, cache_a_v, cache_a_kidx, state_b_k, state_b_v, state_conv, page_table, norm_mix, norm_ffn, w_in_a, q_norm_a, k_norm_a, w_out_a, norm_kv_b, w_kv_b, k_norm_b, w_q_b, q_norm_b, sinks_b, w_out_b, w_up, conv_w, conv_b, w_down):
    bp, seq, _ = x_prompt.shape
    bs, tdec, _ = x_sample.shape
    assert tdec == 1
    n_a = w_in_a.shape[0]
    depth = w_up.shape[0]
    npages = page_table.shape[1]
    past = npages * PAGE
    n_pool = cache_a_k.shape[1]
    tm_p = min(512, seq)
    tm_mlp = min(1024, seq)
    tf = 256

    pos_p = jnp.arange(seq, dtype=I32)
    pos_s = jnp.full((bs,), past, I32)
    tabs = {}
    for name, pos in (("p", pos_p), ("s", pos_s)):
        tabs[name] = (_rope_tables(pos, HD_A), _rope_tables(pos, IDX_DIM),
                      _rope_tables(pos, IDX_DIM, pad_identity=LANES - IDX_DIM))

    row = lambda a: a.reshape(1, -1)
    tile2 = lambda a: jnp.tile(a.reshape(1, -1), (1, 2))
    w_in = jnp.pad(w_in_a, ((0, 0), (0, 0), (0, A_WIDTH_PAD - w_in_a.shape[2]))).astype(BF16)
    w_out_a_b = w_out_a.astype(BF16)
    w_up_b = w_up.astype(BF16)
    w_down_b = w_down.astype(BF16)
    w_kv_b_b = w_kv_b.astype(BF16)
    w_q_b_b = w_q_b.astype(BF16)
    w_out_b_b = w_out_b.astype(BF16)
    ck = cache_a_k.reshape(n_a, n_pool, PAGE * KVH_A, HD_A)
    cv = cache_a_v.reshape(n_a, n_pool, PAGE * KVH_A, HD_A)
    cidx_t = jnp.swapaxes(cache_a_kidx, 2, 3)
    buf_k = state_b_k.reshape(bs, WINDOW, KVH_B * HD_B)
    buf_v = state_b_v.reshape(bs, WINDOW, KVH_B * HD_B)

    hp = x_prompt.reshape(bp * seq, D_MODEL)
    hs = x_sample.reshape(bs, D_MODEL)
    a_k_p, a_v_p, a_ki_p, a_k_s, a_v_s, a_ki_s, conv_p, conv_s = [], [], [], [], [], [], [], []
    topk_s = min(TOPK_MAX, (past + 1) // 4)
    nsplit = 2 if npages % 2 == 0 else 1

    for l in range(depth):
        if l < n_a:
            t128, t64, tki = tabs["p"]
            q, k, v, kb, vt, qi, misc, ki2 = _proj_a(hp, row(norm_mix[l]), w_in[l], row(q_norm_a[l]),
                                                     row(k_norm_a[l]), t128, t64, tki, tm_p)
            o = _dsa_prompt(q, qi, misc, ki2, kb, vt, bp, seq, tm_p)
            o_p, wo_p = o, w_out_a_b[l]
            a_k_p.append(k.reshape(bp, seq, KVH_A, HD_A))
            a_v_p.append(v.reshape(bp, seq, KVH_A, HD_A))
            a_ki_p.append(misc[:, :IDX_DIM].reshape(bp, seq, IDX_DIM))

            t128, t64, tki = tabs["s"]
            q, k, v, kb, vt, qi, misc, ki2 = _proj_a(hs, row(norm_mix[l]), w_in[l], row(q_norm_a[l]),
                                                     row(k_norm_a[l]), t128, t64, tki, bs)
            wi = misc[:, IDX_DIM:IDX_DIM + IDX_HEADS].reshape(bs, IDX_HEADS, 1)
            sc = _s_scores(page_table, qi.reshape(bs, IDX_HEADS, IDX_DIM), wi,
                           ki2[:, :IDX_DIM].reshape(bs, 1, IDX_DIM), cidx_t, l)
            bias = _s_select(sc.reshape(bs, past + LANES), past + 1, topk_s)
            bias_rows = jnp.repeat(bias[:, :past], KVH_A, axis=1)
            o = _s_attn(page_table, q.astype(F32).reshape(bs, H_A, HD_A),
                        bias_rows.reshape(bs, nsplit, 1, past * KVH_A // nsplit),
                        bias[:, past:].reshape(bs, 1, LANES),
                        k.reshape(bs, 1, KVH_A * HD_A), v.reshape(bs, 1, KVH_A * HD_A), ck, cv, l)
            hs = _out_res(o.reshape(bs, H_A * HD_A), w_out_a_b[l], hs, bs)
            a_k_s.append(k.reshape(bs, 1, KVH_A, HD_A))
            a_v_s.append(v.reshape(bs, 1, KVH_A, HD_A))
            a_ki_s.append(misc[:, :IDX_DIM].reshape(bs, 1, IDX_DIM))
        else:
            b = l - n_a
            sink = jnp.broadcast_to(sinks_b[b][:, None], (H_B, LANES))
            sink2 = jnp.repeat(sinks_b[b].reshape(H_B // 2, 2), WINDOW, axis=1)
            qp = _q_b(hp, row(norm_mix[l]), w_q_b_b[b], tile2(q_norm_b[b]), tabs["p"][1], tm_p, BF16)
            o = _swa_prompt(qp, kdup_p, vt_p, sink2, bp, seq)
            o_p, wo_p = o, w_out_b_b[b]
            qs = _q_b(hs, row(norm_mix[l]), w_q_b_b[b], tile2(q_norm_b[b]), tabs["s"][1], bs, F32)
            res = _swa_sample(qs.reshape(bs, H_B, HD_B), buf_k, buf_v, kb_s.reshape(bs, 1, KVH_B * HD_B),
                              vb_s.reshape(bs, 1, KVH_B * HD_B), sink, emit_state=(b == 0))
            if b == 0:
                new_b_k_s, new_b_v_s = res[1], res[2]
            hs = _out_res(res[0].reshape(bs, H_B * HD_B), w_out_b_b[b], hs, bs)

        hp, tail_p = _mlp_prompt(hp, o_p, wo_p, row(norm_ffn[l]), w_up_b[l], conv_w[l], row(conv_b[l]),
                                 w_down_b[l], seq, tm_mlp, tf)
        conv_p.append(tail_p)
        hs, gate_s = _mlp_sample(hs, state_conv[l, :, 0, :], state_conv[l, :, 1, :], row(norm_ffn[l]), w_up_b[l],
                                 conv_w[l], row(conv_b[l]), w_down_b[l], tf)
        conv_s.append(jnp.stack([state_conv[l, :, 1, :], gate_s], axis=1))

        if l == n_a - 1:
            kb_p, vb_p, kdup_p, vt_p = _kv_b(hp, row(norm_kv_b), w_kv_b_b, tile2(k_norm_b), tabs["p"][1], tm_p)
            kb_s, vb_s, _, _ = _kv_b(hs, row(norm_kv_b), w_kv_b_b, tile2(k_norm_b), tabs["s"][1], bs)

    kb_p4 = kb_p.reshape(bp, seq, KVH_B, HD_B)
    vb_p4 = vb_p.reshape(bp, seq, KVH_B, HD_B)
    return (hp.reshape(bp, seq, D_MODEL), hs.reshape(bs, 1, D_MODEL),
            jnp.stack(a_k_p), jnp.stack(a_v_p), jnp.stack(a_ki_p),
            jnp.stack(a_k_s), jnp.stack(a_v_s), jnp.stack(a_ki_s),
            kb_p4[:, -WINDOW:], vb_p4[:, -WINDOW:],
            new_b_k_s.reshape(bs, WINDOW, KVH_B, HD_B), new_b_v_s.reshape(bs, WINDOW, KVH_B, HD_B),
            jnp.stack(conv_p), jnp.stack(conv_s))
```

```python
import functools
import math

import jax
import jax.numpy as jnp
from jax import lax
from jax.experimental import pallas as pl
from jax.experimental.pallas import tpu as pltpu

F32 = jnp.float32
BF16 = jnp.bfloat16
I32 = jnp.int32

D_MODEL = 1024
PAGE = 128
H_A, HD_A, KVH_A = 8, 128, 2
G_A = H_A // KVH_A
IDX_HEADS, IDX_DIM = 16, 64
TOPK_MAX = 256
QB = 128
H_B, HD_B, KVH_B = 16, 64, 2
G_B = H_B // KVH_B
WINDOW = 128
D_FF = 2816
ROPE_THETA = 500000.0
EPS = 1e-6
LANES = 128
SUBLANES = 8
A_Q0, A_K0, A_V0, A_QI0, A_KI0 = 0, 1024, 1280, 1536, 2560
A_WIDTH_PAD = 2688
INT_MIN = -(2 ** 31)
NEG = -1e30
LOG2E = math.log2(math.e)
VMEM_LIMIT = 56 * 1024 * 1024

_NT = (((1,), (1,)), ((), ()))


def _dot(a, b):
    return jnp.dot(a, b, preferred_element_type=F32)


def _dot_nt(a, b):
    return lax.dot_general(a, b, _NT, preferred_element_type=F32)


def _params(sem):
    return pltpu.CompilerParams(dimension_semantics=sem, vmem_limit_bytes=VMEM_LIMIT)


def _resident(shape):
    return pl.BlockSpec(shape, lambda *_: (0,) * len(shape), pipeline_mode=pl.Buffered(1))


def _rms(x, g):
    return x * lax.rsqrt(jnp.mean(x * x, axis=-1, keepdims=True) + EPS) * g


def _head_norm(x, g, hd):
    x2 = x * x
    if hd == LANES:
        r = lax.rsqrt(jnp.sum(x2, axis=-1, keepdims=True) * (1.0 / hd) + EPS)
    else:
        shift = hd.bit_length() - 1
        same = (jnp.right_shift(lax.broadcasted_iota(I32, (LANES, LANES), 0), shift)
                == jnp.right_shift(lax.broadcasted_iota(I32, (LANES, LANES), 1), shift))
        ones_blk = jnp.where(same, 1.0, 0.0).astype(BF16)
        hi = x2.astype(BF16)
        lo = (x2 - hi.astype(F32)).astype(BF16)
        r = lax.rsqrt((_dot(hi, ones_blk) + _dot(lo, ones_blk)) * (1.0 / hd) + EPS)
    return x * r * g


def _rope(x, tab_ref, half):
    c, sa, sb = tab_ref[0], tab_ref[1], tab_ref[2]
    return x * c + pltpu.roll(x, LANES - half, 1) * sa + pltpu.roll(x, half, 1) * sb


def _rope_tables(pos, hd, pad_identity=0):
    rot = hd // 4
    half = rot // 2
    inv = ROPE_THETA ** (-jnp.arange(half, dtype=F32) / half)
    ang = pos.astype(F32)[:, None] * inv[None, :]
    cos, sin = jnp.cos(ang), jnp.sin(ang)
    t = pos.shape[0]
    ones = lambda n: jnp.ones((t, n), F32)
    zeros = lambda n: jnp.zeros((t, n), F32)
    c = jnp.concatenate([cos, cos, ones(hd - rot)], axis=1)
    sa = jnp.concatenate([-sin, zeros(hd - half)], axis=1)
    sb = jnp.concatenate([zeros(half), sin, zeros(hd - rot)], axis=1)
    if pad_identity:
        c = jnp.concatenate([c, ones(pad_identity)], axis=1)
        sa = jnp.concatenate([sa, zeros(pad_identity)], axis=1)
        sb = jnp.concatenate([sb, zeros(pad_identity)], axis=1)
    reps = LANES // c.shape[1]
    return jnp.stack([jnp.tile(a, (1, reps)) for a in (c, sa, sb)])


FLT_LOWEST = float(jnp.finfo(jnp.float32).min)
INT_MAX = 2 ** 31 - 1


def _key_to_float(key):
    return lax.bitcast_convert_type(jnp.where(key >= 0, key, key ^ 0x7FFFFFFF), F32)


def _radix_descend(count_ge, lo, nbits, k):
    def bit_step(i, lo):
        cand = lo ^ jnp.left_shift(jnp.int32(1), nbits - 1 - i)
        return jnp.where(count_ge(cand) >= k, cand, lo)

    return lax.fori_loop(0, nbits, bit_step, lo)


def _fold_lanes(m):
    part = m[:, 0:LANES]
    for t in range(1, m.shape[1] // LANES):
        part = part + m[:, t * LANES:(t + 1) * LANES]
    return part


def _fold_rows(x, op):
    vs = [x[i * SUBLANES:(i + 1) * SUBLANES] for i in range(x.shape[0] // SUBLANES)]
    while len(vs) > 1:
        vs = [op(vs[i], vs[i + 1]) if i + 1 < len(vs) else vs[i] for i in range(0, len(vs), 2)]
    return vs[0]


def _split_pair(blk):
    x = blk.astype(F32)
    lane = lax.broadcasted_iota(I32, x.shape, 1)
    lo = lane < LANES // 2
    both = jnp.concatenate([jnp.where(lo, x, 0.0), jnp.where(lo, 0.0, x)], axis=0)
    return jnp.transpose(both).astype(BF16)


def _proj_a_kernel(x_ref, g_ref, w_ref, qn_ref, kn_ref, t128_ref, t64_ref, tki_ref,
                   q_ref, k_ref, v_ref, kb_ref, vt_ref, qi_ref, misc_ref, ki2_ref):
    xn = _rms(x_ref[...], g_ref[...]).astype(BF16)
    y = _dot(xn, w_ref[...])
    qn, kn = qn_ref[...], kn_ref[...]
    for h in range(H_A):
        xh = _head_norm(y[:, A_Q0 + h * HD_A:A_Q0 + (h + 1) * HD_A], qn, HD_A)
        q_ref[:, h * HD_A:(h + 1) * HD_A] = _rope(xh, t128_ref, HD_A // 8).astype(BF16)
    for n in range(KVH_A):
        xh = _head_norm(y[:, A_K0 + n * HD_A:A_K0 + (n + 1) * HD_A], kn, HD_A)
        kh = _rope(xh, t128_ref, HD_A // 8)
        k_ref[:, n * HD_A:(n + 1) * HD_A] = kh
        kb_ref[:, n * HD_A:(n + 1) * HD_A] = kh.astype(BF16)
    v = y[:, A_V0:A_QI0]
    v_ref[...] = v
    vt_ref[0] = jnp.transpose(v).astype(BF16)
    for c in range(IDX_HEADS * IDX_DIM // LANES):
        xh = y[:, A_QI0 + c * LANES:A_QI0 + (c + 1) * LANES]
        qi_ref[:, c * LANES:(c + 1) * LANES] = _rope(xh, t64_ref, IDX_DIM // 8).astype(BF16)
    m = _rope(y[:, A_KI0:A_WIDTH_PAD], tki_ref, IDX_DIM // 8)
    misc_ref[...] = m
    lane = lax.broadcasted_iota(I32, m.shape, 1)
    ki2_ref[...] = jnp.where(lane < IDX_DIM, m, pltpu.roll(m, IDX_DIM, 1)).astype(BF16)


def _proj_a(h, g, w, qn, kn, t128, t64, tki, tm):
    m = h.shape[0]
    nt = t128.shape[1] // tm
    row = lambda i: (i, 0)
    const = lambda i: (0, 0)
    tab = lambda i: (0, i % nt, 0)
    kvw = KVH_A * HD_A
    out_shapes = (
        jax.ShapeDtypeStruct((m, H_A * HD_A), BF16),
        jax.ShapeDtypeStruct((m, kvw), F32),
        jax.ShapeDtypeStruct((m, kvw), F32),
        jax.ShapeDtypeStruct((m, kvw), BF16),
        jax.ShapeDtypeStruct((m // tm, kvw, tm), BF16),
        jax.ShapeDtypeStruct((m, IDX_HEADS * IDX_DIM), BF16),
        jax.ShapeDtypeStruct((m, LANES), F32),
        jax.ShapeDtypeStruct((m, LANES), BF16),
    )
    out_specs = [pl.BlockSpec((tm, s.shape[1]), row) if len(s.shape) == 2
                 else pl.BlockSpec((1, kvw, tm), lambda i: (i, 0, 0)) for s in out_shapes]
    return pl.pallas_call(
        _proj_a_kernel,
        grid=(m // tm,),
        in_specs=[
            pl.BlockSpec((tm, D_MODEL), row),
            pl.BlockSpec((1, D_MODEL), const),
            _resident((D_MODEL, A_WIDTH_PAD)),
            pl.BlockSpec((1, LANES), const),
            pl.BlockSpec((1, LANES), const),
            pl.BlockSpec((3, tm, LANES), tab),
            pl.BlockSpec((3, tm, LANES), tab),
            pl.BlockSpec((3, tm, LANES), tab),
        ],
        out_specs=out_specs,
        out_shape=out_shapes,
        compiler_params=_params(("parallel",)),
        name="proj_a",
    )(h, g, w, qn, kn, t128, t64, tki)


def _dsa_prompt_kernel(q_ref, qi_ref, w_ref, ki2_ref, kb_ref, vt_ref, o_ref, sc_ref, bias_ref, t_ref, *,
                       ch, topk, pos_bits):
    j = pl.program_id(1)
    t0 = j * QB
    nch = (t0 + QB + ch - 1) // ch
    wt = jnp.transpose(w_ref[...])[IDX_DIM:IDX_DIM + IDX_HEADS, :] * (IDX_DIM ** -0.5 * IDX_HEADS ** -0.5)
    wrows = [wt[h:h + 1, :] for h in range(IDX_HEADS)]
    npair = IDX_HEADS // 2
    pairs = [_split_pair(qi_ref[:, c * LANES:(c + 1) * LANES]) for c in range(npair)]
    krow = lax.broadcasted_iota(I32, (ch, QB), 0)
    qcol = lax.broadcasted_iota(I32, (ch, QB), 1) + t0

    def score_chunk(c, carry):
        ks = pl.multiple_of(c * ch, ch)
        ki2 = ki2_ref[pl.ds(ks, ch), :]
        acc = jnp.zeros((ch, QB), F32)
        for cc in range(npair):
            d = _dot(ki2, pairs[cc])
            acc = acc + jnp.maximum(d[:, :QB], 0.0) * wrows[2 * cc] + jnp.maximum(d[:, QB:], 0.0) * wrows[2 * cc + 1]
        sc_ref[c] = jnp.where(krow + ks <= qcol, acc, -jnp.inf)
        return carry

    lax.fori_loop(0, nch, score_chunk, 0)

    def count_ge(cand):
        cand_f = _key_to_float(cand)

        def body(c, cnt):
            return cnt + _fold_rows(jnp.where(sc_ref[c] >= cand_f, 1.0, 0.0), jnp.add)

        cnt = lax.fori_loop(0, nch, body, jnp.zeros((SUBLANES, QB), F32))
        return jnp.sum(cnt, axis=0, keepdims=True)

    thr_key = _radix_descend(count_ge, jnp.full((1, QB), INT_MIN, I32), 32, topk)
    short = lax.broadcasted_iota(I32, (1, QB), 1) + t0 + 1 < topk
    thr = jnp.where(short, FLT_LOWEST, _key_to_float(thr_key))

    def bias_chunk(c, carry):
        bias_ref[c] = jnp.where(sc_ref[c] >= thr, 0.0, NEG)
        return carry

    lax.fori_loop(0, nch, bias_chunk, 0)

    tied = (count_ge(thr_key) > topk) & jnp.logical_not(short)

    @pl.when(jnp.max(tied.astype(I32)) > 0)
    def _():
        above_key = jnp.where(thr_key == INT_MAX, thr_key, thr_key + 1)
        above = jnp.where(short, FLT_LOWEST, _key_to_float(above_key))
        need = topk - count_ge(above_key)

        def band_excess(c):
            s = sc_ref[c]
            band = (s >= thr) & (s < above)
            return band, jnp.where(band, s - thr, -jnp.inf)

        def count_excess_ge(cand):
            cand_f = _key_to_float(cand)

            def body(c, cnt):
                return cnt + _fold_rows(jnp.where(band_excess(c)[1] >= cand_f, 1.0, 0.0), jnp.add)

            cnt = lax.fori_loop(0, nch, body, jnp.zeros((SUBLANES, QB), F32))
            return jnp.sum(cnt, axis=0, keepdims=True)

        ex_key = _radix_descend(count_excess_ge, jnp.full((1, QB), INT_MIN, I32), 32, need)
        ex_thr = _key_to_float(ex_key)
        need_pos = need - count_excess_ge(jnp.where(ex_key == INT_MAX, ex_key, ex_key + 1))

        def tied_before(pos):
            def body(c, cnt):
                hit = (band_excess(c)[1] == ex_thr) & (krow + c * ch < pos)
                return cnt + _fold_rows(jnp.where(hit, 1.0, 0.0), jnp.add)

            cnt = lax.fori_loop(0, nch, body, jnp.zeros((SUBLANES, QB), F32))
            return jnp.sum(cnt, axis=0, keepdims=True)

        def pos_step(i, pos):
            cand = pos | jnp.left_shift(jnp.int32(1), pos_bits - 1 - i)
            return jnp.where(tied_before(cand) < need_pos, cand, pos)

        last = lax.fori_loop(0, pos_bits, pos_step, jnp.zeros((1, QB), I32))

        def fix_chunk(c, carry):
            band, ex = band_excess(c)
            in_band = band & ((ex > ex_thr) | ((ex == ex_thr) & (krow + c * ch <= last)))
            bias_ref[c] = jnp.where((sc_ref[c] >= above) | in_band, 0.0, NEG)
            return carry

        lax.fori_loop(0, nch, fix_chunk, 0)

    cfac = HD_A ** -0.5 * LOG2E
    qns = [jnp.concatenate([jnp.transpose(q_ref[:, (n * G_A + g) * HD_A:(n * G_A + g + 1) * HD_A].astype(F32))
                            for g in range(G_A)], axis=1).astype(BF16) for n in range(KVH_A)]

    def logits_chunk(c, mxs):
        ks = pl.multiple_of(c * ch, ch)
        b4 = jnp.concatenate([bias_ref[c]] * G_A, axis=1)
        out = []
        for n in range(KVH_A):
            kc = kb_ref[pl.ds(ks, ch), n * HD_A:(n + 1) * HD_A]
            t = _dot(kc, qns[n]) * cfac + b4
            t_ref[n, c] = t
            out.append(jnp.maximum(mxs[n], _fold_rows(t, jnp.maximum)))
        return tuple(out)

    mxs = lax.fori_loop(0, nch, logits_chunk, (jnp.full((SUBLANES, G_A * QB), NEG, F32),) * KVH_A)
    ms = [jnp.max(mx, axis=0, keepdims=True) for mx in mxs]

    def pv_chunk(c, carry):
        out = []
        for n in range(KVH_A):
            lsum, acc = carry[n]
            p = jnp.exp2(t_ref[n, c] - ms[n])
            acc = acc + _dot(vt_ref[c, n * HD_A:(n + 1) * HD_A, :], p.astype(BF16))
            out.append((lsum + _fold_rows(p, jnp.add), acc))
        return tuple(out)

    init = ((jnp.zeros((SUBLANES, G_A * QB), F32), jnp.zeros((HD_A, G_A * QB), F32)),) * KVH_A
    res = lax.fori_loop(0, nch, pv_chunk, init)
    for n in range(KVH_A):
        lsum, acc = res[n]
        out = acc / jnp.sum(lsum, axis=0, keepdims=True)
        for g in range(G_A):
            h = n * G_A + g
            o_ref[:, h * HD_A:(h + 1) * HD_A] = jnp.transpose(out[:, g * QB:(g + 1) * QB]).astype(BF16)


def _dsa_prompt(q, qi, misc, ki2, kb, vt, nbatch, seq, ch):
    nq = seq // QB
    nchunks = seq // ch
    topk = min(TOPK_MAX, seq // 4)
    kvw = KVH_A * HD_A
    qblk = lambda b, j: (b * nq + j, 0)
    bat = lambda b, j: (b, 0)
    return pl.pallas_call(
        functools.partial(_dsa_prompt_kernel, ch=ch, topk=topk, pos_bits=(seq - 1).bit_length()),
        grid=(nbatch, nq),
        in_specs=[
            pl.BlockSpec((QB, H_A * HD_A), qblk),
            pl.BlockSpec((QB, IDX_HEADS * IDX_DIM), qblk),
            pl.BlockSpec((QB, LANES), qblk),
            pl.BlockSpec((seq, LANES), bat),
            pl.BlockSpec((seq, kvw), bat),
            pl.BlockSpec((nchunks, kvw, ch), lambda b, j: (b, 0, 0)),
        ],
        out_specs=pl.BlockSpec((QB, H_A * HD_A), qblk),
        out_shape=jax.ShapeDtypeStruct((nbatch * seq, H_A * HD_A), BF16),
        scratch_shapes=[pltpu.VMEM((nchunks, ch, QB), F32), pltpu.VMEM((nchunks, ch, QB), F32),
                        pltpu.VMEM((KVH_A, nchunks, ch, G_A * QB), F32)],
        compiler_params=_params(("parallel", "parallel")),
        name="dsa_prompt",
    )(q, qi, misc, ki2, kb, vt)


def _out_res_kernel(o_ref, w_ref, h_ref, y_ref):
    y_ref[...] = h_ref[...] + _dot(o_ref[...].astype(BF16), w_ref[...])


def _out_res(o, w, h, tm):
    m, kdim = o.shape
    row = lambda i: (i, 0)
    return pl.pallas_call(
        _out_res_kernel,
        grid=(m // tm,),
        in_specs=[pl.BlockSpec((tm, kdim), row), _resident((kdim, D_MODEL)),
                  pl.BlockSpec((tm, D_MODEL), row)],
        out_specs=pl.BlockSpec((tm, D_MODEL), row),
        out_shape=jax.ShapeDtypeStruct((m, D_MODEL), F32),
        compiler_params=_params(("parallel",)),
        name="out_res",
    )(o, w, h)


HALO = 16


def _silu(x):
    return x * jax.nn.sigmoid(x)


def _mlp_prompt_kernel(h_ref, halo_ref, o_ref, ohalo_ref, wo_ref, g_ref, wup_ref, cw_ref, cb_ref, wd_ref,
                       out_ref, tail_ref, xn_ref, gs_ref, act_ref, *, tm, tf, tiles_per_seq):
    i = pl.program_id(0)
    g = g_ref[...]
    wo = wo_ref[...]
    xn_ref[0:HALO, :] = _rms(halo_ref[...] + _dot(ohalo_ref[...], wo), g).astype(BF16)
    h1 = h_ref[...] + _dot(o_ref[...], wo)
    out_ref[...] = h1
    xn_ref[HALO:HALO + tm, :] = _rms(h1, g).astype(BF16)
    first = i % tiles_per_seq == 0
    for f in range(D_FF // tf):
        cols = slice(f * tf, (f + 1) * tf)
        gs_ref[...] = _dot(xn_ref[...], wup_ref[:, cols])
        up = _dot(xn_ref[HALO:HALO + tm, :], wup_ref[:, D_FF + f * tf:D_FF + (f + 1) * tf])

        @pl.when(first)
        def _():
            gs_ref[0:HALO, :] = jnp.zeros((HALO, tf), F32)

        conv = (cb_ref[:, cols] + cw_ref[0:1, cols] * gs_ref[HALO - 2:HALO - 2 + tm, :]
                + cw_ref[1:2, cols] * gs_ref[HALO - 1:HALO - 1 + tm, :]
                + cw_ref[2:3, cols] * gs_ref[HALO:HALO + tm, :])
        act_ref[:, cols] = (_silu(conv) * up).astype(BF16)
        tail_ref[0, :, cols] = gs_ref[tm:tm + HALO, :]
    out_ref[...] += _dot(act_ref[...], wd_ref[...])


def _mlp_prompt(h, o, w_o, g, w_up, cw, cb, w_down, seq, tm, tf):
    m = h.shape[0]
    kdim = o.shape[1]
    tiles_per_seq = seq // tm
    hb = tm // HALO
    row = lambda i: (i, 0)
    halo = lambda i: (jnp.maximum(i * hb - 1, 0), 0)
    out, tails = pl.pallas_call(
        functools.partial(_mlp_prompt_kernel, tm=tm, tf=tf, tiles_per_seq=tiles_per_seq),
        grid=(m // tm,),
        in_specs=[
            pl.BlockSpec((tm, D_MODEL), row),
            pl.BlockSpec((HALO, D_MODEL), halo),
            pl.BlockSpec((tm, kdim), row),
            pl.BlockSpec((HALO, kdim), halo),
            _resident((kdim, D_MODEL)),
            _resident((1, D_MODEL)),
            _resident((D_MODEL, 2 * D_FF)),
            _resident((3, D_FF)),
            _resident((1, D_FF)),
            _resident((D_FF, D_MODEL)),
        ],
        out_specs=[pl.BlockSpec((tm, D_MODEL), lambda i: (i, 0)),
                   pl.BlockSpec((1, HALO, D_FF), lambda i: (i, 0, 0))],
        out_shape=(jax.ShapeDtypeStruct((m, D_MODEL), F32),
                   jax.ShapeDtypeStruct((m // tm, HALO, D_FF), F32)),
        scratch_shapes=[pltpu.VMEM((tm + HALO, D_MODEL), BF16), pltpu.VMEM((tm + HALO, tf), F32),
                        pltpu.VMEM((tm, D_FF), BF16)],
        compiler_params=_params(("parallel",)),
        name="mlp_prompt",
    )(h, h, o, o, w_o, g, w_up, cw, cb, w_down)
    nb = m // seq
    tail = tails.reshape(nb, tiles_per_seq, HALO, D_FF)[:, -1, HALO - 2:, :]
    return out, tail


def _mlp_sample_kernel(h_ref, s0_ref, s1_ref, g_ref, wg_ref, wu_ref, cw_ref, cb_ref, wd_ref,
                       out_ref, gate_ref, xn_ref, acc_ref):
    j = pl.program_id(0)

    @pl.when(j == 0)
    def _():
        xn_ref[...] = _rms(h_ref[...], g_ref[...]).astype(BF16)
        acc_ref[...] = jnp.zeros_like(acc_ref)

    xn = xn_ref[...]
    gate = _dot(xn, wg_ref[...])
    up = _dot(xn, wu_ref[...])
    gate_ref[...] = gate
    conv = cb_ref[...] + cw_ref[0:1, :] * s0_ref[...] + cw_ref[1:2, :] * s1_ref[...] + cw_ref[2:3, :] * gate
    act = (_silu(conv) * up).astype(BF16)
    acc_ref[...] += _dot(act, wd_ref[...])

    @pl.when(j == pl.num_programs(0) - 1)
    def _():
        out_ref[...] = h_ref[...] + acc_ref[...]


def _mlp_sample(h, s0, s1, g, w_up, cw, cb, w_down, tf):
    m = h.shape[0]
    nf = D_FF // tf
    full = lambda j: (0, 0)
    col = lambda j: (0, j)
    return pl.pallas_call(
        _mlp_sample_kernel,
        grid=(nf,),
        in_specs=[
            pl.BlockSpec((m, D_MODEL), full),
            pl.BlockSpec((m, tf), col),
            pl.BlockSpec((m, tf), col),
            pl.BlockSpec((1, D_MODEL), full),
            pl.BlockSpec((D_MODEL, tf), col),
            pl.BlockSpec((D_MODEL, tf), lambda j: (0, nf + j)),
            pl.BlockSpec((3, tf), col),
            pl.BlockSpec((1, tf), col),
            pl.BlockSpec((tf, D_MODEL), lambda j: (j, 0)),
        ],
        out_specs=[pl.BlockSpec((m, D_MODEL), full), pl.BlockSpec((m, tf), col)],
        out_shape=(jax.ShapeDtypeStruct((m, D_MODEL), F32), jax.ShapeDtypeStruct((m, D_FF), F32)),
        scratch_shapes=[pltpu.VMEM((m, D_MODEL), BF16), pltpu.VMEM((m, D_MODEL), F32)],
        compiler_params=_params(("arbitrary",)),
        name="mlp_sample",
    )(h, s0, s1, g, w_up, w_up, cw, cb, w_down)


def _kv_b_kernel(x_ref, g_ref, w_ref, kn_ref, t64_ref, k_ref, v_ref, kdup_ref, vt_ref):
    xn = _rms(x_ref[...], g_ref[...]).astype(BF16)
    y = _dot(xn, w_ref[...])
    k = _rope(_head_norm(y[:, :LANES], kn_ref[...], HD_B), t64_ref, HD_B // 8)
    v = y[:, LANES:]
    k_ref[...] = k
    v_ref[...] = v
    lane = lax.broadcasted_iota(I32, k.shape, 1)
    kr = pltpu.roll(k, HD_B, 1)
    kdup_ref[:, :LANES] = jnp.where(lane < HD_B, k, kr).astype(BF16)
    kdup_ref[:, LANES:] = jnp.where(lane < HD_B, kr, k).astype(BF16)
    for s in range(vt_ref.shape[0]):
        vt_ref[s] = jnp.transpose(v[s * WINDOW:(s + 1) * WINDOW, :]).astype(BF16)


def _kv_b(h, g, w, kn, t64, tm):
    m = h.shape[0]
    nt = t64.shape[1] // tm
    row = lambda i: (i, 0)
    const = lambda i: (0, 0)
    width = KVH_B * HD_B
    nsub = max(tm // WINDOW, 1)
    vt_rows = min(tm, WINDOW)
    shapes = (jax.ShapeDtypeStruct((m, width), F32), jax.ShapeDtypeStruct((m, width), F32),
              jax.ShapeDtypeStruct((m, 2 * width), BF16),
              jax.ShapeDtypeStruct((m // vt_rows, width, vt_rows), BF16))
    return pl.pallas_call(
        _kv_b_kernel,
        grid=(m // tm,),
        in_specs=[pl.BlockSpec((tm, D_MODEL), row), pl.BlockSpec((1, D_MODEL), const),
                  _resident((D_MODEL, 2 * width)), pl.BlockSpec((1, LANES), const),
                  pl.BlockSpec((3, tm, LANES), lambda i: (0, i % nt, 0))],
        out_specs=[pl.BlockSpec((tm, width), row), pl.BlockSpec((tm, width), row),
                   pl.BlockSpec((tm, 2 * width), row),
                   pl.BlockSpec((nsub, width, vt_rows), lambda i: (i, 0, 0))],
        out_shape=shapes,
        compiler_params=_params(("parallel",)),
        name="kv_b",
    )(h, g, w, kn, t64)


def _q_b_kernel(x_ref, g_ref, w_ref, qn_ref, t64_ref, q_ref):
    xn = _rms(x_ref[...], g_ref[...]).astype(BF16)
    y = _dot(xn, w_ref[...])
    qn = qn_ref[...]
    for c in range(H_B * HD_B // LANES):
        xh = _head_norm(y[:, c * LANES:(c + 1) * LANES], qn, HD_B)
        q_ref[:, c * LANES:(c + 1) * LANES] = (_rope(xh, t64_ref, HD_B // 8) * (HD_B ** -0.5)).astype(q_ref.dtype)


def _q_b(h, g, w, qn, t64, tm, dtype):
    m = h.shape[0]
    nt = t64.shape[1] // tm
    row = lambda i: (i, 0)
    const = lambda i: (0, 0)
    return pl.pallas_call(
        _q_b_kernel,
        grid=(m // tm,),
        in_specs=[pl.BlockSpec((tm, D_MODEL), row), pl.BlockSpec((1, D_MODEL), const),
                  _resident((D_MODEL, H_B * HD_B)), pl.BlockSpec((1, LANES), const),
                  pl.BlockSpec((3, tm, LANES), lambda i: (0, i % nt, 0))],
        out_specs=pl.BlockSpec((tm, H_B * HD_B), row),
        out_shape=jax.ShapeDtypeStruct((m, H_B * HD_B), dtype),
        compiler_params=_params(("parallel",)),
        name="q_b",
    )(h, g, w, qn, t64)


def _swa_prompt_kernel(q_ref, kc_ref, kp_ref, vtc_ref, vtp_ref, sink_ref, o_ref):
    j = pl.program_id(1)
    kr = lax.broadcasted_iota(I32, (2 * WINDOW, WINDOW), 0)
    qc = lax.broadcasted_iota(I32, (2 * WINDOW, WINDOW), 1)
    rel = kr - WINDOW - qc
    ok = (rel <= 0) & (rel >= -WINDOW) & ((kr >= WINDOW) | (j > 0))
    bias = jnp.where(ok, 0.0, NEG)
    bias2 = jnp.concatenate([bias, bias], axis=1)
    kband = jnp.concatenate([kp_ref[...], kc_ref[...]], axis=0)
    vt = jnp.concatenate([vtp_ref[0], vtc_ref[0]], axis=1)
    npair = H_B // 2
    kv_of = [(2 * c) // G_B for c in range(npair)]
    logits = [_dot(kband[:, kv_of[c] * LANES:(kv_of[c] + 1) * LANES],
                   _split_pair(q_ref[:, c * LANES:(c + 1) * LANES])) + bias2 for c in range(npair)]
    probs, rdenom = [], []
    for c in range(npair):
        sink = sink_ref[c:c + 1, :]
        m = jnp.maximum(jnp.max(logits[c], axis=0, keepdims=True), sink)
        p = jnp.exp(logits[c] - m)
        rdenom.append(1.0 / (jnp.sum(p, axis=0, keepdims=True) + jnp.exp(sink - m)))
        probs.append(p.astype(BF16))
    outs = [_dot(vt[kv_of[c] * HD_B:(kv_of[c] + 1) * HD_B, :], probs[c]) * rdenom[c]
            for c in range(npair)]
    for c in range(npair):
        stacked = jnp.concatenate([outs[c][:, :WINDOW], outs[c][:, WINDOW:]], axis=0)
        o_ref[:, c * LANES:(c + 1) * LANES] = jnp.transpose(stacked).astype(BF16)


def _swa_prompt(q, kdup, vt, sink2, nbatch, seq):
    nblk = seq // WINDOW
    cur = lambda b, j: (b * nblk + j, 0)
    prev = lambda b, j: (b * nblk + jnp.maximum(j - 1, 0), 0)
    cur3 = lambda b, j: (b * nblk + j, 0, 0)
    prev3 = lambda b, j: (b * nblk + jnp.maximum(j - 1, 0), 0, 0)
    width = KVH_B * HD_B
    return pl.pallas_call(
        _swa_prompt_kernel,
        grid=(nbatch, nblk),
        in_specs=[pl.BlockSpec((WINDOW, H_B * HD_B), cur),
                  pl.BlockSpec((WINDOW, 2 * width), cur), pl.BlockSpec((WINDOW, 2 * width), prev),
                  pl.BlockSpec((1, width, WINDOW), cur3), pl.BlockSpec((1, width, WINDOW), prev3),
                  pl.BlockSpec((H_B // 2, 2 * WINDOW), lambda b, j: (0, 0))],
        out_specs=pl.BlockSpec((WINDOW, H_B * HD_B), cur),
        out_shape=jax.ShapeDtypeStruct((nbatch * seq, H_B * HD_B), BF16),
        compiler_params=_params(("parallel", "parallel")),
        name="swa_prompt",
    )(q, kdup, kdup, vt, vt, sink2)


SWA_SAMPLE_ROWS = 8


def _swa_sample_kernel(q_ref, bk_ref, bv_ref, kn_ref, vn_ref, sink_ref, o_ref, *state_refs):
    rowh = lax.broadcasted_iota(I32, (H_B, 1), 0)
    first = rowh < G_B
    sink = sink_ref[:, 0:1]
    for i in range(q_ref.shape[0]):
        q = q_ref[i]
        bk, bv = bk_ref[i], bv_ref[i]
        knew, vnew = kn_ref[i], vn_ref[i]
        qb = q.astype(BF16)
        s_past = jnp.where(first, _dot_nt(qb, bk[:, :HD_B].astype(BF16)), _dot_nt(qb, bk[:, HD_B:].astype(BF16)))
        knew_h = jnp.where(first, knew[:, :HD_B], knew[:, HD_B:])
        vnew_h = jnp.where(first, vnew[:, :HD_B], vnew[:, HD_B:])
        s_new = jnp.sum(q * knew_h, axis=-1, keepdims=True)
        m = jnp.maximum(jnp.maximum(jnp.max(s_past, axis=-1, keepdims=True), s_new), sink)
        p_past = jnp.exp(s_past - m)
        p_new = jnp.exp(s_new - m)
        denom = jnp.sum(p_past, axis=-1, keepdims=True) + p_new + jnp.exp(sink - m)
        pb = (p_past / denom).astype(BF16)
        o_past = jnp.where(first, _dot(pb, bv[:, :HD_B].astype(BF16)), _dot(pb, bv[:, HD_B:].astype(BF16)))
        o_ref[i] = o_past + (p_new / denom) * vnew_h
        if state_refs:
            nk_ref, nv_ref = state_refs
            nk_ref[i] = jnp.concatenate([bk[1:], knew], axis=0)
            nv_ref[i] = jnp.concatenate([bv[1:], vnew], axis=0)


def _swa_sample(q, buf_k, buf_v, k_new, v_new, sinks, emit_state):
    nb = q.shape[0]
    width = KVH_B * HD_B
    rows = SWA_SAMPLE_ROWS if nb % SWA_SAMPLE_ROWS == 0 else 1
    blk3 = lambda *s: pl.BlockSpec((rows,) + s, lambda b: (b, 0, 0))
    out_shape = [jax.ShapeDtypeStruct((nb, H_B, HD_B), F32)]
    out_specs = [blk3(H_B, HD_B)]
    if emit_state:
        out_shape += [jax.ShapeDtypeStruct((nb, WINDOW, width), F32)] * 2
        out_specs += [blk3(WINDOW, width)] * 2
    return pl.pallas_call(
        _swa_sample_kernel,
        grid=(nb // rows,),
        in_specs=[blk3(H_B, HD_B), blk3(WINDOW, width), blk3(WINDOW, width), blk3(1, width), blk3(1, width),
                  pl.BlockSpec((H_B, LANES), lambda b: (0, 0))],
        out_specs=out_specs,
        out_shape=out_shape,
        compiler_params=_params(("parallel",)),
        name="swa_sample",
    )(q, buf_k, buf_v, k_new, v_new, sinks)


def _s_scores_kernel(pt_ref, qi_ref, w_ref, kin_ref, cache_ref, out_ref, buf_ref, sem, *, layer, npages):
    b = pl.program_id(0)
    nb = pl.num_programs(0)
    slot = b % 2

    def copies(bb, sl):
        return [pltpu.make_async_copy(cache_ref.at[layer, pt_ref[bb, p]],
                                      buf_ref.at[sl, :, pl.ds(p * PAGE, PAGE)], sem.at[sl])
                for p in range(npages)]

    @pl.when(b == 0)
    def _():
        for cp in copies(0, 0):
            cp.start()

    @pl.when(b + 1 < nb)
    def _():
        for cp in copies(b + 1, 1 - slot):
            cp.start()

    for cp in copies(b, slot):
        cp.wait()

    qi = qi_ref[0]
    w = w_ref[0] * (IDX_DIM ** -0.5 * IDX_HEADS ** -0.5)
    past = npages * PAGE
    d_past = _dot(qi, buf_ref[slot].astype(BF16))
    d_new = _dot_nt(qi, jnp.broadcast_to(kin_ref[0], (LANES, IDX_DIM)))
    out_ref[0, :, 0:past] = jnp.sum(jnp.maximum(d_past, 0.0) * w, axis=0, keepdims=True)
    out_ref[0, :, past:past + LANES] = jnp.sum(jnp.maximum(d_new, 0.0) * w, axis=0, keepdims=True)


def _s_scores(page_table, qi, wi, ki_new, cache_t, layer):
    nb, npages = page_table.shape
    past = npages * PAGE
    blk3 = lambda *s: pl.BlockSpec((1,) + s, lambda b, pt: (b, 0, 0))
    return pl.pallas_call(
        functools.partial(_s_scores_kernel, layer=layer, npages=npages),
        grid_spec=pltpu.PrefetchScalarGridSpec(
            num_scalar_prefetch=1,
            grid=(nb,),
            in_specs=[blk3(IDX_HEADS, IDX_DIM), blk3(IDX_HEADS, 1), blk3(1, IDX_DIM),
                      pl.BlockSpec(memory_space=pl.ANY)],
            out_specs=blk3(1, past + LANES),
            scratch_shapes=[pltpu.VMEM((2, IDX_DIM, past), F32), pltpu.SemaphoreType.DMA((2,))],
        ),
        out_shape=jax.ShapeDtypeStruct((nb, 1, past + LANES), F32),
        compiler_params=_params(("arbitrary",)),
        name="sample_scores",
    )(page_table, qi, wi, ki_new, cache_t)


def _s_select_kernel(sc_ref, bias_ref, key_ref, *, nvalid, topk, ch):
    rows, width = sc_ref.shape
    nch = width // ch
    for c in range(nch):
        lane = lax.broadcasted_iota(I32, (rows, ch), 1) + c * ch
        key_ref[:, c * ch:(c + 1) * ch] = jnp.where(lane < nvalid, sc_ref[:, c * ch:(c + 1) * ch], -jnp.inf)

    def count_ge(cand):
        cand_f = _key_to_float(cand)
        cnt = jnp.zeros((rows, LANES), F32)
        for c in range(nch):
            cnt = cnt + _fold_lanes(jnp.where(key_ref[:, c * ch:(c + 1) * ch] >= cand_f, 1.0, 0.0))
        return jnp.sum(cnt, axis=1, keepdims=True)

    thr_key = _radix_descend(count_ge, jnp.full((rows, 1), INT_MIN, I32), 32, topk)
    thr = jnp.full((rows, 1), FLT_LOWEST, F32) if nvalid < topk else _key_to_float(thr_key)
    for c in range(nch):
        bias_ref[:, c * ch:(c + 1) * ch] = jnp.where(key_ref[:, c * ch:(c + 1) * ch] >= thr, 0.0, NEG)

    tied = (count_ge(thr_key) > topk) & (nvalid >= topk)

    @pl.when(jnp.max(tied.astype(I32)) > 0)
    def _():
        above_key = jnp.where(thr_key == INT_MAX, thr_key, thr_key + 1)
        above = _key_to_float(above_key)
        need = topk - count_ge(above_key)
        lanes = [lax.broadcasted_iota(I32, (rows, ch), 1) + c * ch for c in range(nch)]

        def band_excess(c):
            s = key_ref[:, c * ch:(c + 1) * ch]
            band = (s >= thr) & (s < above)
            return band, jnp.where(band, s - thr, -jnp.inf)

        def count_excess_ge(cand):
            cand_f = _key_to_float(cand)
            cnt = jnp.zeros((rows, LANES), F32)
            for c in range(nch):
                cnt = cnt + _fold_lanes(jnp.where(band_excess(c)[1] >= cand_f, 1.0, 0.0))
            return jnp.sum(cnt, axis=1, keepdims=True)

        ex_key = _radix_descend(count_excess_ge, jnp.full((rows, 1), INT_MIN, I32), 32, need)
        ex_thr = _key_to_float(ex_key)
        need_pos = need - count_excess_ge(jnp.where(ex_key == INT_MAX, ex_key, ex_key + 1))

        def tied_before(pos):
            cnt = jnp.zeros((rows, LANES), F32)
            for c in range(nch):
                hit = (band_excess(c)[1] == ex_thr) & (lanes[c] < pos)
                cnt = cnt + _fold_lanes(jnp.where(hit, 1.0, 0.0))
            return jnp.sum(cnt, axis=1, keepdims=True)

        pos_bits = (width - 1).bit_length()

        def pos_step(i, pos):
            cand = pos | jnp.left_shift(jnp.int32(1), pos_bits - 1 - i)
            return jnp.where(tied_before(cand) < need_pos, cand, pos)

        last = lax.fori_loop(0, pos_bits, pos_step, jnp.zeros((rows, 1), I32))
        for c in range(nch):
            band, ex = band_excess(c)
            in_band = band & ((ex > ex_thr) | ((ex == ex_thr) & (lanes[c] <= last)))
            bias_ref[:, c * ch:(c + 1) * ch] = jnp.where((key_ref[:, c * ch:(c + 1) * ch] >= above) | in_band, 0.0, NEG)


def _s_select(sc, nvalid, topk):
    rows, width = sc.shape
    ch = LANES * 5 if width % (LANES * 5) == 0 else LANES
    return pl.pallas_call(
        functools.partial(_s_select_kernel, nvalid=nvalid, topk=topk, ch=ch),
        out_shape=jax.ShapeDtypeStruct((rows, width), F32),
        scratch_shapes=[pltpu.VMEM((rows, width), F32)],
        compiler_params=pltpu.CompilerParams(vmem_limit_bytes=VMEM_LIMIT),
        name="sample_select",
    )(sc)


def _s_attn_kernel(pt_ref, q_ref, bias_ref, bself_ref, kn_ref, vn_ref, kc_ref, vc_ref, o_ref,
                   kbuf, vbuf, ksem, vsem, m_ref, l_ref, acc_ref, *, layer, pages_per_step, nsplit):
    b = pl.program_id(0)
    hf = pl.program_id(1)
    step = b * nsplit + hf
    nsteps = pl.num_programs(0) * nsplit
    slot = step % 2
    prow = PAGE * KVH_A

    def copies(st, sl):
        bb = st // nsplit
        p0 = (st % nsplit) * pages_per_step
        out = []
        for cache, buf, sem in ((kc_ref, kbuf, ksem), (vc_ref, vbuf, vsem)):
            out += [pltpu.make_async_copy(cache.at[layer, pt_ref[bb, p0 + p]],
                                          buf.at[sl, pl.ds(p * prow, prow)], sem.at[sl])
                    for p in range(pages_per_step)]
        return out

    @pl.when(step == 0)
    def _():
        for cp in copies(0, 0):
            cp.start()

    @pl.when(step + 1 < nsteps)
    def _():
        for cp in copies(step + 1, 1 - slot):
            cp.start()

    for cp in copies(step, slot):
        cp.wait()

    @pl.when(hf == 0)
    def _():
        m_ref[...] = jnp.full(m_ref.shape, NEG, F32)
        l_ref[...] = jnp.zeros_like(l_ref)
        acc_ref[...] = jnp.zeros_like(acc_ref)

    scale = HD_A ** -0.5
    q = q_ref[0]
    nrow = pages_per_step * prow
    head_kv = jnp.right_shift(lax.broadcasted_iota(I32, (H_A, nrow), 0), G_A.bit_length() - 1)
    col_kv = jnp.bitwise_and(lax.broadcasted_iota(I32, (H_A, nrow), 1), KVH_A - 1)
    s = _dot_nt(q.astype(BF16), kbuf[slot].astype(BF16)) * scale + bias_ref[0, 0]
    s = jnp.where(head_kv == col_kv, s, NEG)
    m = m_ref[...]
    m_new = jnp.maximum(m, jnp.max(s, axis=-1, keepdims=True))
    p = jnp.exp(s - m_new)
    alpha = jnp.exp(m - m_new)
    l_ref[...] = alpha * l_ref[...] + jnp.sum(p, axis=-1, keepdims=True)
    acc_ref[...] = alpha * acc_ref[...] + _dot(p.astype(BF16), vbuf[slot].astype(BF16))
    m_ref[...] = m_new

    @pl.when(hf == nsplit - 1)
    def _():
        first = lax.broadcasted_iota(I32, (H_A, 1), 0) < G_A
        knew, vnew = kn_ref[0], vn_ref[0]
        knew_h = jnp.where(first, knew[:, :HD_A], knew[:, HD_A:])
        vnew_h = jnp.where(first, vnew[:, :HD_A], vnew[:, HD_A:])
        s_new = jnp.sum(q * knew_h, axis=-1, keepdims=True) * scale + bself_ref[0, :, 0:1]
        m_old = m_ref[...]
        m_fin = jnp.maximum(m_old, s_new)
        a = jnp.exp(m_old - m_fin)
        p_new = jnp.exp(s_new - m_fin)
        l = a * l_ref[...] + p_new
        o_ref[0] = (a * acc_ref[...] + p_new * vnew_h) / l


def _s_attn(page_table, q, bias_past, bias_self, k_new, v_new, cache_k, cache_v, layer):
    nb, npages = page_table.shape
    nsplit = bias_past.shape[1]
    pps = npages // nsplit
    width = KVH_A * HD_A
    nrow = pps * PAGE * KVH_A
    blk3 = lambda *s: pl.BlockSpec((1,) + s, lambda b, h, pt: (b, 0, 0))
    return pl.pallas_call(
        functools.partial(_s_attn_kernel, layer=layer, pages_per_step=pps, nsplit=nsplit),
        grid_spec=pltpu.PrefetchScalarGridSpec(
            num_scalar_prefetch=1,
            grid=(nb, nsplit),
            in_specs=[blk3(H_A, HD_A),
                      pl.BlockSpec((1, 1, 1, nrow), lambda b, h, pt: (b, h, 0, 0)),
                      blk3(1, LANES), blk3(1, width), blk3(1, width),
                      pl.BlockSpec(memory_space=pl.ANY), pl.BlockSpec(memory_space=pl.ANY)],
            out_specs=blk3(H_A, HD_A),
            scratch_shapes=[pltpu.VMEM((2, nrow, HD_A), F32), pltpu.VMEM((2, nrow, HD_A), F32),
                            pltpu.SemaphoreType.DMA((2,)), pltpu.SemaphoreType.DMA((2,)),
                            pltpu.VMEM((H_A, 1), F32), pltpu.VMEM((H_A, 1), F32), pltpu.VMEM((H_A, HD_A), F32)],
        ),
        out_shape=jax.ShapeDtypeStruct((nb, H_A, HD_A), F32),
        compiler_params=_params(("arbitrary", "arbitrary")),
        name="sample_attn",
    )(page_table, q, bias_past, bias_self, k_new, v_new, cache_k, cache_v)


def kernel(x_prompt, x_sample, cache_a_k, cache_a_v, cache_a_kidx, state_b_k, state_b_v, state_conv, page_table, norm_mix, norm_ffn, w_in_a, q_norm_a, k_norm_a, w_out_a, norm_kv_b, w_kv_b, k_norm_b, w_q_b, q_norm_b, sinks_b, w_out_b, w_up, conv_w, conv_b, w_down):
    bp, seq, _ = x_prompt.shape
    bs, tdec, _ = x_sample.shape
    assert tdec == 1
    n_a = w_in_a.shape[0]
    depth = w_up.shape[0]
    npages = page_table.shape[1]
    past = npages * PAGE
    n_pool = cache_a_k.shape[1]
    tm_p = min(512, seq)
    tm_mlp = min(1024, seq)
    tf = 256

    pos_p = jnp.arange(seq, dtype=I32)
    pos_s = jnp.full((bs,), past, I32)
    tabs = {}
    for name, pos in (("p", pos_p), ("s", pos_s)):
        tabs[name] = (_rope_tables(pos, HD_A), _rope_tables(pos, IDX_DIM),
                      _rope_tables(pos, IDX_DIM, pad_identity=LANES - IDX_DIM))

    row = lambda a: a.reshape(1, -1)
    tile2 = lambda a: jnp.tile(a.reshape(1, -1), (1, 2))
    w_in = jnp.pad(w_in_a, ((0, 0), (0, 0), (0, A_WIDTH_PAD - w_in_a.shape[2]))).astype(BF16)
    w_out_a_b = w_out_a.astype(BF16)
    w_up_b = w_up.astype(BF16)
    w_down_b = w_down.astype(BF16)
    w_kv_b_b = w_kv_b.astype(BF16)
    w_q_b_b = w_q_b.astype(BF16)
    w_out_b_b = w_out_b.astype(BF16)
    ck = cache_a_k.reshape(n_a, n_pool, PAGE * KVH_A, HD_A)
    cv = cache_a_v.reshape(n_a, n_pool, PAGE * KVH_A, HD_A)
    cidx_t = jnp.swapaxes(cache_a_kidx, 2, 3)
    buf_k = state_b_k.reshape(bs, WINDOW, KVH_B * HD_B)
    buf_v = state_b_v.reshape(bs, WINDOW, KVH_B * HD_B)

    hp = x_prompt.reshape(bp * seq, D_MODEL)
    hs = x_sample.reshape(bs, D_MODEL)
    a_k_p, a_v_p, a_ki_p, a_k_s, a_v_s, a_ki_s, conv_p, conv_s = [], [], [], [], [], [], [], []
    topk_s = min(TOPK_MAX, (past + 1) // 4)
    nsplit = 2 if npages % 2 == 0 else 1

    for l in range(depth):
        if l < n_a:
            t128, t64, tki = tabs["p"]
            q, k, v, kb, vt, qi, misc, ki2 = _proj_a(hp, row(norm_mix[l]), w_in[l], row(q_norm_a[l]),
                                                     row(k_norm_a[l]), t128, t64, tki, tm_p)
            o = _dsa_prompt(q, qi, misc, ki2, kb, vt, bp, seq, tm_p)
            o_p, wo_p = o, w_out_a_b[l]
            a_k_p.append(k.reshape(bp, seq, KVH_A, HD_A))
            a_v_p.append(v.reshape(bp, seq, KVH_A, HD_A))
            a_ki_p.append(misc[:, :IDX_DIM].reshape(bp, seq, IDX_DIM))

            t128, t64, tki = tabs["s"]
            q, k, v, kb, vt, qi, misc, ki2 = _proj_a(hs, row(norm_mix[l]), w_in[l], row(q_norm_a[l]),
                                                     row(k_norm_a[l]), t128, t64, tki, bs)
            wi = misc[:, IDX_DIM:IDX_DIM + IDX_HEADS].reshape(bs, IDX_HEADS, 1)
            sc = _s_scores(page_table, qi.reshape(bs, IDX_HEADS, IDX_DIM), wi,
                           ki2[:, :IDX_DIM].reshape(bs, 1, IDX_DIM), cidx_t, l)
            bias = _s_select(sc.reshape(bs, past + LANES), past + 1, topk_s)
            bias_rows = jnp.repeat(bias[:, :past], KVH_A, axis=1)
            o = _s_attn(page_table, q.astype(F32).reshape(bs, H_A, HD_A),
                        bias_rows.reshape(bs, nsplit, 1, past * KVH_A // nsplit),
                        bias[:, past:].reshape(bs, 1, LANES),
                        k.reshape(bs, 1, KVH_A * HD_A), v.reshape(bs, 1, KVH_A * HD_A), ck, cv, l)
            hs = _out_res(o.reshape(bs, H_A * HD_A), w_out_a_b[l], hs, bs)
            a_k_s.append(k.reshape(bs, 1, KVH_A, HD_A))
            a_v_s.append(v.reshape(bs, 1, KVH_A, HD_A))
            a_ki_s.append(misc[:, :IDX_DIM].reshape(bs, 1, IDX_DIM))
        else:
            b = l - n_a
            sink = jnp.broadcast_to(sinks_b[b][:, None], (H_B, LANES))
            sink2 = jnp.repeat(sinks_b[b].reshape(H_B // 2, 2), WINDOW, axis=1)
            qp = _q_b(hp, row(norm_mix[l]), w_q_b_b[b], tile2(q_norm_b[b]), tabs["p"][1], tm_p, BF16)
            o = _swa_prompt(qp, kdup_p, vt_p, sink2, bp, seq)
            o_p, wo_p = o, w_out_b_b[b]
            qs = _q_b(hs, row(norm_mix[l]), w_q_b_b[b], tile2(q_norm_b[b]), tabs["s"][1], bs, F32)
            res = _swa_sample(qs.reshape(bs, H_B, HD_B), buf_k, buf_v, kb_s.reshape(bs, 1, KVH_B * HD_B),
                              vb_s.reshape(bs, 1, KVH_B * HD_B), sink, emit_state=(b == 0))
            if b == 0:
                new_b_k_s, new_b_v_s = res[1], res[2]
            hs = _out_res(res[0].reshape(bs, H_B * HD_B), w_out_b_b[b], hs, bs)

        hp, tail_p = _mlp_prompt(hp, o_p, wo_p, row(norm_ffn[l]), w_up_b[l], conv_w[l], row(conv_b[l]),
                                 w_down_b[l], seq, tm_mlp, tf)
        conv_p.append(tail_p)
        hs, gate_s = _mlp_sample(hs, state_conv[l, :, 0, :], state_conv[l, :, 1, :], row(norm_ffn[l]), w_up_b[l],
                                 conv_w[l], row(conv_b[l]), w_down_b[l], tf)
        conv_s.append(jnp.stack([state_conv[l, :, 1, :], gate_s], axis=1))

        if l == n_a - 1:
            kb_p, vb_p, kdup_p, vt_p = _kv_b(hp, row(norm_kv_b), w_kv_b_b, tile2(k_norm_b), tabs["p"][1], tm_p)
            kb_s, vb_s, _, _ = _kv_b(hs, row(norm_kv_b), w_kv_b_b, tile2(k_norm_b), tabs["s"][1], bs)

    kb_p4 = kb_p.reshape(bp, seq, KVH_B, HD_B)
    vb_p4 = vb_p.reshape(bp, seq, KVH_B, HD_B)
    return (hp.reshape(bp, seq, D_MODEL), hs.reshape(bs, 1, D_MODEL),
            jnp.stack(a_k_p), jnp.stack(a_v_p), jnp.stack(a_ki_p),
            jnp.stack(a_k_s), jnp.stack(a_v_s), jnp.stack(a_ki_s),
            kb_p4[:, -WINDOW:], vb_p4[:, -WINDOW:],
            new_b_k_s.reshape(bs, WINDOW, KVH_B, HD_B), new_b_v_s.reshape(bs, WINDOW, KVH_B, HD_B),
            jnp.stack(conv_p), jnp.stack(conv_s))
```
